```python
import math
import jax, jax.numpy as jnp
from jax import lax
import numpy as np

D_MODEL = 1024
BATCH = 2
SEQ = 8192
DEPTH = 4
DEC_BATCH = 128
DEC_SEQ = 4
PAST_LEN = 8192
PAGE_SIZE = 128

HEAD_DIM = 64
H_A = 8
G_A = 2
R_A = H_A // G_A
L_CMP = 32
L_SEL = 64
N_SEL = 16
WINDOW = 512
H_B = 8
G_B = 2
R_B = H_B // G_B
FORGET_BIAS = 5.0
H_C = 16
Q_LORA = 512
KV_LORA = 256
D_NOPE = 64
D_ROPE = 32
D_V = 64
ROPE_THETA = 10000.0
MLA_SCALE = (D_NOPE + D_ROPE) ** -0.5
N_BUCKETS = 32
MAX_DISTANCE = 128
N_EXPERTS = 32
TOP_K = 4
D_FF = D_MODEL
SWIGLU_LIMIT = 7.0
SWIGLU_ALPHA = 1.702

Q_BLOCK = 128
RMS_EPS = 1e-6
N_EVEN = (DEPTH + 1) // 2
N_ODD = DEPTH // 2
A_Q = H_A * HEAD_DIM
A_KV = 2 * G_A * HEAD_DIM
A_GATE = 3 * H_A
B_Q = H_B * HEAD_DIM
B_KV = 2 * G_B * HEAD_DIM
B_F = H_B
SPLITS_AB = (A_Q, A_Q + A_KV, A_Q + 2 * A_KV, A_Q + 3 * A_KV, A_Q + 3 * A_KV + A_GATE,
             A_Q + 3 * A_KV + A_GATE + B_Q, A_Q + 3 * A_KV + A_GATE + B_Q + B_KV)
E_IN_AB = SPLITS_AB[-1] + B_F
E_IN_C = Q_LORA + KV_LORA + D_ROPE

kernel_name = 'hybrid_nsa_fox_mla_moe_step'


def rmsnorm(x, g):
    xf = x.astype(jnp.float32)
    y = xf * lax.rsqrt(jnp.mean(xf * xf, axis=-1, keepdims=True) + RMS_EPS)
    return (y * g.astype(jnp.float32)).astype(x.dtype)


def masked_softmax(logits, mask):
    logits = jnp.where(mask, logits, -jnp.inf)
    m = jnp.max(logits, axis=-1, keepdims=True)
    m = jnp.where(jnp.isfinite(m), m, 0.0)
    e = jnp.exp(logits - m)
    s = jnp.sum(e, axis=-1, keepdims=True)
    return e / jnp.where(s > 0, s, 1.0)


def t5_bucket(dist):
    n = jnp.maximum(dist, 0)
    max_exact = N_BUCKETS // 2
    nf = jnp.maximum(n, 1).astype(jnp.float32)
    large = max_exact + (jnp.log(nf / max_exact) / math.log(MAX_DISTANCE / max_exact)
                         * (N_BUCKETS - max_exact)).astype(jnp.int32)
    return jnp.where(n < max_exact, n, jnp.minimum(large, N_BUCKETS - 1))


def t5_bias(table, qpos, kpos):
    b = table[t5_bucket(qpos[:, None] - kpos[None, :])]
    return jnp.transpose(b, (2, 0, 1)).reshape(G_A, R_A, qpos.shape[0], kpos.shape[0]).astype(jnp.float32)


def rope(x, pos):
    half = D_ROPE // 2
    inv = ROPE_THETA ** (-jnp.arange(half, dtype=jnp.float32) / half)
    ang = pos.astype(jnp.float32)[:, None] * inv[None, :]
    ang = ang.reshape((ang.shape[0],) + (1,) * (x.ndim - 3) + (half,))
    cos, sin = jnp.cos(ang), jnp.sin(ang)
    xf = x.astype(jnp.float32)
    x1, x2 = xf[..., :half], xf[..., half:]
    return jnp.concatenate([x1 * cos - x2 * sin, x1 * sin + x2 * cos], axis=-1).astype(x.dtype)


def ada_modulation(c, w_ada, b_ada):
    m = jax.nn.silu(c) @ w_ada + b_ada
    return jnp.split(m[:, None, :], 6, axis=-1)


def modulate(x, g, shift, scale):
    return rmsnorm(x, g) * (1 + scale) + shift


def moe_ffn(h, w_router, b_router, w_gu, b_gu, w_down, b_down):
    x = h.reshape(-1, D_MODEL)
    logits = (x @ w_router + b_router).astype(jnp.float32)
    top_val, top_idx = lax.top_k(logits, TOP_K)
    top_w = jax.nn.softmax(top_val, axis=-1)
    gates = jnp.sum(jax.nn.one_hot(top_idx, N_EXPERTS, dtype=jnp.float32) * top_w[..., None], axis=1)

    def expert_step(acc, ew):
        w_gu_e, b_gu_e, w_dn_e, b_dn_e, g_e = ew
        gu = x @ w_gu_e + b_gu_e
        gate = jnp.minimum(gu[:, :D_FF], SWIGLU_LIMIT)
        up = jnp.clip(gu[:, D_FF:], -SWIGLU_LIMIT, SWIGLU_LIMIT)
        act = (up + 1) * gate * jax.nn.sigmoid(SWIGLU_ALPHA * gate)
        y = act @ w_dn_e + b_dn_e
        return acc + g_e[:, None].astype(y.dtype) * y, None

    out, _ = lax.scan(expert_step, jnp.zeros_like(x), (w_gu, b_gu, w_down, b_down, gates.T))
    return out.reshape(h.shape)


def even_project(h, w_in, b_forget):
    B_, T_ = h.shape[0], h.shape[1]
    q_a, kv_cmp, kv_sel, kv_win, g_a, q_b, kv_b, f_b = jnp.split(h @ w_in, SPLITS_AB, axis=-1)
    kv_shape = (B_, T_, 2, G_A, HEAD_DIM)
    return (q_a.reshape(B_, T_, G_A, R_A, HEAD_DIM),
            kv_cmp.reshape(kv_shape), kv_sel.reshape(kv_shape), kv_win.reshape(kv_shape),
            jax.nn.sigmoid(g_a.astype(jnp.float32)).reshape(B_, T_, 3, H_A),
            q_b.reshape(B_, T_, G_B, R_B, HEAD_DIM),
            kv_b.reshape(B_, T_, 2, G_B, HEAD_DIM),
            jax.nn.log_sigmoid((f_b + b_forget).astype(jnp.float32)))


def compress_blocks(kv_rows, w_cmp, pe_cmp):
    B_, S_ = kv_rows.shape[0], kv_rows.shape[1]
    blk = kv_rows.reshape(B_, S_ // L_CMP, L_CMP, 2, G_A, HEAD_DIM)
    blk = blk + jnp.transpose(pe_cmp, (1, 0, 2))[None, None, :, :, None, :]
    return jnp.einsum('bnlkgd,kldo->bnkgo', blk, w_cmp)


def nsa_attend(q, gates, qpos, ck, cv, gather_sel, win_k, win_v, win_pos, t5_table):
    B_, Q_ = q.shape[0], q.shape[1]
    scale = HEAD_DIM ** -0.5
    nc = ck.shape[1]
    c_end = (jnp.arange(nc, dtype=jnp.int32) + 1) * L_CMP - 1
    lg = jnp.einsum('bqgrd,bngd->bgrqn', q, ck, preferred_element_type=jnp.float32) * scale
    lg = lg + t5_bias(t5_table, qpos, c_end)[None]
    p_cmp = masked_softmax(lg, c_end[None, :] <= qpos[:, None])
    o_cmp = jnp.einsum('bgrqn,bngd->bqgrd', p_cmp.astype(cv.dtype), cv)
    ns = nc * L_CMP // L_SEL
    imp = jnp.sum(p_cmp, axis=2).reshape(B_, G_A, Q_, ns, L_SEL // L_CMP).sum(-1)
    blk = jnp.arange(ns, dtype=jnp.int32)[None, :]
    cur = (qpos // L_SEL)[:, None]
    valid = blk <= cur
    forced = (blk == 0) | (blk == cur) | (blk == cur - 1)
    score = jnp.where(forced & valid, jnp.inf, jnp.where(valid, imp, -jnp.inf))
    k_sel = min(N_SEL, ns)
    _, idx = lax.top_k(score, k_sel)
    ks, vs = gather_sel(idx)
    ks = ks.reshape(B_, G_A, Q_, k_sel * L_SEL, HEAD_DIM)
    vs = vs.reshape(B_, G_A, Q_, k_sel * L_SEL, HEAD_DIM)
    kpos = (idx[..., None] * L_SEL + jnp.arange(L_SEL, dtype=jnp.int32)).reshape(B_, G_A, Q_, k_sel * L_SEL)
    dist = qpos[None, None, :, None] - kpos
    t5_g = jnp.transpose(t5_table.reshape(N_BUCKETS, G_A, R_A), (1, 0, 2))
    bias = t5_g[jnp.arange(G_A)[None, :, None, None], t5_bucket(dist)]
    lg = jnp.einsum('bqgrd,bgqkd->bgrqk', q, ks, preferred_element_type=jnp.float32) * scale
    lg = lg + jnp.moveaxis(bias, -1, 2).astype(jnp.float32)
    p = masked_softmax(lg, (dist >= 0)[:, :, None])
    o_sel = jnp.einsum('bgrqk,bgqkd->bqgrd', p.astype(vs.dtype), vs)
    wdist = qpos[:, None] - win_pos[None, :]
    lg = jnp.einsum('bqgrd,bwgd->bgrqw', q, win_k, preferred_element_type=jnp.float32) * scale
    lg = lg + t5_bias(t5_table, qpos, win_pos)[None]
    p = masked_softmax(lg, (wdist >= 0) & (wdist < WINDOW) & (win_pos[None, :] >= 0))
    o_win = jnp.einsum('bgrqw,bwgd->bqgrd', p.astype(win_v.dtype), win_v)
    g = gates.reshape(B_, Q_, 3, G_A, R_A, 1).astype(q.dtype)
    return g[:, :, 0] * o_cmp + g[:, :, 1] * o_sel + g[:, :, 2] * o_win


def fox_attend(q, cq, qpos, k, v, ck, kpos):
    B_, Q_ = q.shape[0], q.shape[1]
    S_ = k.shape[1]
    lg = jnp.einsum('bqgrd,bsgd->bgrqs', q, k, preferred_element_type=jnp.float32) * HEAD_DIM ** -0.5
    dq = jnp.transpose(cq.reshape(B_, Q_, G_B, R_B), (0, 2, 3, 1))[..., None]
    dk = jnp.transpose(ck.reshape(B_, S_, G_B, R_B), (0, 2, 3, 1))[..., None, :]
    p = masked_softmax(lg + (dq - dk), kpos[None, :] <= qpos[:, None])
    return jnp.einsum('bgrqs,bsgd->bqgrd', p.astype(v.dtype), v)


def mla_project(h, pos, w_in, g_q, w_qb, g_kv, w_kvb):
    B_, T_ = h.shape[0], h.shape[1]
    cq, ckv, kpe = jnp.split(h @ w_in, (Q_LORA, Q_LORA + KV_LORA), axis=-1)
    q = (rmsnorm(cq, g_q) @ w_qb).reshape(B_, T_, H_C, D_NOPE + D_ROPE)
    q_pe = rope(q[..., D_NOPE:], pos)
    w_uk = w_kvb.reshape(KV_LORA, H_C, D_NOPE + D_V)[..., :D_NOPE]
    q_lat = jnp.einsum('bthn,chn->bthc', q[..., :D_NOPE], w_uk)
    return q_lat, q_pe, rmsnorm(ckv, g_kv), rope(kpe, pos)


def mla_attend(q_lat, q_pe, qpos, ckv, kpe, kpos):
    lg = (jnp.einsum('bqhc,bsc->bhqs', q_lat, ckv, preferred_element_type=jnp.float32)
          + jnp.einsum('bqhr,bsr->bhqs', q_pe, kpe, preferred_element_type=jnp.float32)) * MLA_SCALE
    p = masked_softmax(lg, kpos[None, :] <= qpos[:, None])
    return jnp.einsum('bhqs,bsc->bqhc', p.astype(ckv.dtype), ckv)


def mla_output(o_lat, w_kvb, w_out):
    B_, T_ = o_lat.shape[0], o_lat.shape[1]
    w_uv = w_kvb.reshape(KV_LORA, H_C, D_NOPE + D_V)[..., D_NOPE:]
    o = jnp.einsum('bthc,chv->bthv', o_lat, w_uv)
    return o.reshape(B_, T_, H_C * D_V) @ w_out


def to_blocks(a, n_blk):
    return jnp.moveaxis(a.reshape((a.shape[0], n_blk, Q_BLOCK) + a.shape[2:]), 1, 0)


def gather_pages(pool, page_table):
    g = pool[page_table]
    return g.reshape((page_table.shape[0], page_table.shape[1] * PAGE_SIZE) + pool.shape[2:])


def even_mixer_prompt(h, w_in, b_f, w_cmp, pe_cmp, w_out, t5_table):
    B_, T_ = h.shape[0], h.shape[1]
    q_a, kv_cmp, kv_sel, kv_win, gates, q_b, kv_b, logf = even_project(h, w_in, b_f)
    pos = jnp.arange(T_, dtype=jnp.int32)
    cmp_blk = compress_blocks(kv_cmp, w_cmp, pe_cmp)
    ck, cv = cmp_blk[:, :, 0], cmp_blk[:, :, 1]
    sel_blocks = kv_sel.reshape(B_, T_ // L_SEL, L_SEL, 2, G_A, HEAD_DIM)
    bi = jnp.arange(B_)[:, None, None, None]
    gi = jnp.arange(G_A)[None, :, None, None]

    def gather_sel(idx):
        blk = sel_blocks[bi, idx, :, :, gi]
        return blk[..., 0, :], blk[..., 1, :]

    win_pad = jnp.pad(kv_win, ((0, 0), (WINDOW, 0), (0, 0), (0, 0), (0, 0)))
    cum_f = jnp.cumsum(logf, axis=1)
    k_b, v_b = kv_b[:, :, 0], kv_b[:, :, 1]
    n_blk = T_ // Q_BLOCK

    def block_fn(args):
        i, qa_i, g_i, qb_i, cq_i = args
        q0 = i * Q_BLOCK
        qpos = q0 + jnp.arange(Q_BLOCK, dtype=jnp.int32)
        win = lax.dynamic_slice_in_dim(win_pad, q0, WINDOW + Q_BLOCK, axis=1)
        wpos = q0 - WINDOW + jnp.arange(WINDOW + Q_BLOCK, dtype=jnp.int32)
        o_a = nsa_attend(qa_i, g_i, qpos, ck, cv, gather_sel, win[:, :, 0], win[:, :, 1], wpos, t5_table)
        o_b = fox_attend(qb_i, cq_i, qpos, k_b, v_b, cum_f, pos)
        return jnp.concatenate([o_a.reshape(B_, Q_BLOCK, A_Q), o_b.reshape(B_, Q_BLOCK, B_Q)], axis=-1)

    o = lax.map(block_fn, (jnp.arange(n_blk, dtype=jnp.int32), to_blocks(q_a, n_blk),
                           to_blocks(gates, n_blk), to_blocks(q_b, n_blk), to_blocks(cum_f, n_blk)))
    y = jnp.moveaxis(o, 0, 1).reshape(B_, T_, A_Q + B_Q) @ w_out
    n_keep = min(WINDOW, T_)
    return y, (kv_cmp, kv_sel, kv_b, logf, kv_win[:, T_ - n_keep:])


def even_mixer_decode(h, pool_cmp, pool_sel, pool_fox_kv, pool_fox_logf, win_buf, page_table,
                      w_in, b_f, w_cmp, pe_cmp, w_out, t5_table):
    B_, T_ = h.shape[0], h.shape[1]
    q_a, kv_cmp, kv_sel, kv_win, gates, q_b, kv_b, logf = even_project(h, w_in, b_f)
    past = page_table.shape[1] * PAGE_SIZE
    total = past + T_
    padded = -(-total // L_SEL) * L_SEL
    qpos = past + jnp.arange(T_, dtype=jnp.int32)
    kpos = jnp.arange(total, dtype=jnp.int32)
    rows_cmp = jnp.concatenate([gather_pages(pool_cmp, page_table), kv_cmp], axis=1)
    rows_cmp = jnp.pad(rows_cmp, ((0, 0), (0, padded - total), (0, 0), (0, 0), (0, 0)))
    cmp_blk = compress_blocks(rows_cmp, w_cmp, pe_cmp)
    ck, cv = cmp_blk[:, :, 0], cmp_blk[:, :, 1]
    bpp = PAGE_SIZE // L_SEL
    n_past_blk = past // L_SEL
    pool_blocks = pool_sel.reshape((-1, L_SEL) + pool_sel.shape[2:])
    new_len = padded - past
    new_blocks = jnp.pad(kv_sel, ((0, 0), (0, new_len - T_), (0, 0), (0, 0), (0, 0)))
    new_blocks = new_blocks.reshape(B_, new_len // L_SEL, L_SEL, 2, G_A, HEAD_DIM)
    bi = jnp.arange(B_)[:, None, None, None]
    gi = jnp.arange(G_A)[None, :, None, None]

    def gather_sel(idx):
        jp = jnp.minimum(idx, n_past_blk - 1)
        phys = page_table[bi, jp // bpp] * bpp + jp % bpp
        from_pool = pool_blocks[phys, :, :, gi]
        from_new = new_blocks[bi, jnp.maximum(idx - n_past_blk, 0), :, :, gi]
        blk = jnp.where((idx < n_past_blk)[..., None, None, None], from_pool, from_new)
        return blk[..., 0, :], blk[..., 1, :]

    wb = win_buf.shape[1]
    win_rows = jnp.concatenate([win_buf, kv_win], axis=1)
    wpos = past - wb + jnp.arange(wb + T_, dtype=jnp.int32)
    o_a = nsa_attend(q_a, gates, qpos, ck, cv, gather_sel, win_rows[:, :, 0], win_rows[:, :, 1], wpos, t5_table)
    logf_all = jnp.concatenate([gather_pages(pool_fox_logf, page_table).astype(jnp.float32), logf], axis=1)
    cum_all = jnp.cumsum(logf_all, axis=1)
    kv_all = jnp.concatenate([gather_pages(pool_fox_kv, page_table), kv_b], axis=1)
    o_b = fox_attend(q_b, cum_all[:, past:], qpos, kv_all[:, :, 0], kv_all[:, :, 1], cum_all, kpos)
    o = jnp.concatenate([o_a.reshape(B_, T_, A_Q), o_b.reshape(B_, T_, B_Q)], axis=-1)
    return o @ w_out, (kv_cmp, kv_sel, kv_b, logf, win_rows[:, T_:])


def odd_mixer_prompt(h, w_in, g_q, w_qb, g_kv, w_kvb, w_out):
    B_, T_ = h.shape[0], h.shape[1]
    pos = jnp.arange(T_, dtype=jnp.int32)
    q_lat, q_pe, ckv, kpe = mla_project(h, pos, w_in, g_q, w_qb, g_kv, w_kvb)
    n_blk = T_ // Q_BLOCK

    def block_fn(args):
        i, ql_i, qp_i = args
        qpos = i * Q_BLOCK + jnp.arange(Q_BLOCK, dtype=jnp.int32)
        return mla_attend(ql_i, qp_i, qpos, ckv, kpe, pos)

    o_lat = lax.map(block_fn, (jnp.arange(n_blk, dtype=jnp.int32), to_blocks(q_lat, n_blk), to_blocks(q_pe, n_blk)))
    o_lat = jnp.moveaxis(o_lat, 0, 1).reshape(B_, T_, H_C, KV_LORA)
    return mla_output(o_lat, w_kvb, w_out), (ckv, kpe)


def odd_mixer_decode(h, pool_ckv, pool_kpe, page_table, w_in, g_q, w_qb, g_kv, w_kvb, w_out):
    T_ = h.shape[1]
    past = page_table.shape[1] * PAGE_SIZE
    qpos = past + jnp.arange(T_, dtype=jnp.int32)
    q_lat, q_pe, ckv, kpe = mla_project(h, qpos, w_in, g_q, w_qb, g_kv, w_kvb)
    ckv_all = jnp.concatenate([gather_pages(pool_ckv, page_table), ckv], axis=1)
    kpe_all = jnp.concatenate([gather_pages(pool_kpe, page_table), kpe], axis=1)
    o_lat = mla_attend(q_lat, q_pe, qpos, ckv_all, kpe_all, jnp.arange(past + T_, dtype=jnp.int32))
    return mla_output(o_lat, w_kvb, w_out), (ckv, kpe)


def setup_inputs(seed: int = 0) -> dict:
    key = jax.random.key(seed)
    keys = iter(jax.random.split(key, 40))
    f32 = jnp.float32
    n_pages = PAST_LEN // PAGE_SIZE
    n_used = DEC_BATCH * n_pages
    n_pool = n_used + max(1, n_used // 4)
    win_buf = min(WINDOW, PAST_LEN)

    def nrm(shape, scale=1.0):
        return jax.random.normal(next(keys), shape, f32) * scale

    inp = {}
    inp['x_prompt'] = nrm((BATCH, SEQ, D_MODEL))
    inp['x_sample'] = nrm((DEC_BATCH, DEC_SEQ, D_MODEL))
    inp['c_prompt'] = nrm((BATCH, D_MODEL))
    inp['c_sample'] = nrm((DEC_BATCH, D_MODEL))
    inp['cache_nsa_cmp'] = nrm((N_EVEN, n_pool, PAGE_SIZE, 2, G_A, HEAD_DIM))
    inp['cache_nsa_sel'] = nrm((N_EVEN, n_pool, PAGE_SIZE, 2, G_A, HEAD_DIM))
    inp['cache_fox_kv'] = nrm((N_EVEN, n_pool, PAGE_SIZE, 2, G_B, HEAD_DIM))
    inp['cache_fox_logf'] = jax.nn.log_sigmoid(nrm((N_EVEN, n_pool, PAGE_SIZE, H_B)) + FORGET_BIAS)
    inp['state_nsa_win'] = nrm((N_EVEN, DEC_BATCH, win_buf, 2, G_A, HEAD_DIM))
    inp['cache_mla_ckv'] = nrm((N_ODD, n_pool, PAGE_SIZE, KV_LORA))
    inp['cache_mla_kpe'] = nrm((N_ODD, n_pool, PAGE_SIZE, D_ROPE))
    inp['page_table'] = jax.random.permutation(next(keys), n_pool)[:n_used].reshape(DEC_BATCH, n_pages).astype(jnp.int32)
    inp['t5_table'] = nrm((N_BUCKETS, H_A), 0.5)
    inp['w_in_ab'] = nrm((N_EVEN, D_MODEL, E_IN_AB), D_MODEL ** -0.5)
    inp['b_forget'] = FORGET_BIAS + nrm((N_EVEN, H_B), 0.5)
    inp['w_cmp'] = nrm((N_EVEN, 2, L_CMP, HEAD_DIM, HEAD_DIM), (L_CMP * HEAD_DIM) ** -0.5)
    inp['pe_cmp'] = nrm((N_EVEN, 2, L_CMP, HEAD_DIM), 0.1)
    inp['w_out_ab'] = nrm((N_EVEN, A_Q + B_Q, D_MODEL), (A_Q + B_Q) ** -0.5)
    inp['w_in_c'] = nrm((N_ODD, D_MODEL, E_IN_C), D_MODEL ** -0.5)
    inp['g_qnorm'] = 1.0 + nrm((N_ODD, Q_LORA), 0.05)
    inp['w_qb'] = nrm((N_ODD, Q_LORA, H_C * (D_NOPE + D_ROPE)), Q_LORA ** -0.5)
    inp['g_kvnorm'] = 1.0 + nrm((N_ODD, KV_LORA), 0.05)
    inp['w_kvb'] = nrm((N_ODD, KV_LORA, H_C * (D_NOPE + D_V)), KV_LORA ** -0.5)
    inp['w_out_c'] = nrm((N_ODD, H_C * D_V, D_MODEL), (H_C * D_V) ** -0.5)
    inp['g_norm_mix'] = 1.0 + nrm((DEPTH, D_MODEL), 0.05)
    inp['g_norm_ffn'] = 1.0 + nrm((DEPTH, D_MODEL), 0.05)
    inp['w_ada'] = nrm((DEPTH, D_MODEL, 6 * D_MODEL), 0.5 * D_MODEL ** -0.5)
    inp['b_ada'] = nrm((DEPTH, 6 * D_MODEL), 0.02)
    inp['w_router'] = nrm((DEPTH, D_MODEL, N_EXPERTS), D_MODEL ** -0.5)
    inp['b_router'] = nrm((DEPTH, N_EXPERTS), 0.01)
    inp['w_moe_gu'] = nrm((DEPTH, N_EXPERTS, D_MODEL, 2 * D_FF), D_MODEL ** -0.5)
    inp['b_moe_gu'] = nrm((DEPTH, N_EXPERTS, 2 * D_FF), 0.01)
    inp['w_moe_down'] = nrm((DEPTH, N_EXPERTS, D_FF, D_MODEL), D_FF ** -0.5)
    inp['b_moe_down'] = nrm((DEPTH, N_EXPERTS, D_MODEL), 0.01)
    inp['g_final'] = 1.0 + nrm((D_MODEL,), 0.05)
    return inp


def reference(x_prompt, x_sample, c_prompt, c_sample,
              cache_nsa_cmp, cache_nsa_sel, cache_fox_kv, cache_fox_logf, state_nsa_win,
              cache_mla_ckv, cache_mla_kpe, page_table,
              t5_table, w_in_ab, b_forget, w_cmp, pe_cmp, w_out_ab,
              w_in_c, g_qnorm, w_qb, g_kvnorm, w_kvb, w_out_c,
              g_norm_mix, g_norm_ffn, w_ada, b_ada,
              w_router, b_router, w_moe_gu, b_moe_gu, w_moe_down, b_moe_down, g_final):
    xp, xs = x_prompt, x_sample
    even_p, even_s, odd_p, odd_s = [], [], [], []
    for layer in range(DEPTH):
        mp = ada_modulation(c_prompt, w_ada[layer], b_ada[layer])
        ms = ada_modulation(c_sample, w_ada[layer], b_ada[layer])
        hp = modulate(xp, g_norm_mix[layer], mp[0], mp[1])
        hs = modulate(xs, g_norm_mix[layer], ms[0], ms[1])
        if layer % 2 == 0:
            e = layer // 2
            yp, st_p = even_mixer_prompt(hp, w_in_ab[e], b_forget[e], w_cmp[e], pe_cmp[e], w_out_ab[e], t5_table)
            ys, st_s = even_mixer_decode(hs, cache_nsa_cmp[e], cache_nsa_sel[e], cache_fox_kv[e], cache_fox_logf[e],
                                         state_nsa_win[e], page_table, w_in_ab[e], b_forget[e], w_cmp[e],
                                         pe_cmp[e], w_out_ab[e], t5_table)
            even_p.append(st_p)
            even_s.append(st_s)
        else:
            o = layer // 2
            yp, st_p = odd_mixer_prompt(hp, w_in_c[o], g_qnorm[o], w_qb[o], g_kvnorm[o], w_kvb[o], w_out_c[o])
            ys, st_s = odd_mixer_decode(hs, cache_mla_ckv[o], cache_mla_kpe[o], page_table, w_in_c[o], g_qnorm[o],
                                        w_qb[o], g_kvnorm[o], w_kvb[o], w_out_c[o])
            odd_p.append(st_p)
            odd_s.append(st_s)
        xp = xp + mp[2] * yp
        xs = xs + ms[2] * ys
        hp = modulate(xp, g_norm_ffn[layer], mp[3], mp[4])
        hs = modulate(xs, g_norm_ffn[layer], ms[3], ms[4])
        xp = xp + mp[5] * moe_ffn(hp, w_router[layer], b_router[layer], w_moe_gu[layer], b_moe_gu[layer],
                                  w_moe_down[layer], b_moe_down[layer])
        xs = xs + ms[5] * moe_ffn(hs, w_router[layer], b_router[layer], w_moe_gu[layer], b_moe_gu[layer],
                                  w_moe_down[layer], b_moe_down[layer])
    y_prompt = rmsnorm(xp, g_final)
    y_sample = rmsnorm(xs, g_final)
    nsa_cmp_p, nsa_sel_p, fox_kv_p, fox_logf_p, win_p = [jnp.stack(a) for a in zip(*even_p)]
    nsa_cmp_s, nsa_sel_s, fox_kv_s, fox_logf_s, win_s = [jnp.stack(a) for a in zip(*even_s)]
    mla_ckv_p, mla_kpe_p = [jnp.stack(a) for a in zip(*odd_p)]
    mla_ckv_s, mla_kpe_s = [jnp.stack(a) for a in zip(*odd_s)]
    return (y_prompt, y_sample, nsa_cmp_p, nsa_cmp_s, nsa_sel_p, nsa_sel_s, fox_kv_p, fox_kv_s,
            fox_logf_p, fox_logf_s, win_p, win_s, mla_ckv_p, mla_ckv_s, mla_kpe_p, mla_kpe_s)
```

```python
import functools
import math

import numpy as np
import jax
import jax.numpy as jnp
from jax import lax
from jax.experimental import pallas as pl
from jax.experimental.pallas import tpu as pltpu

F32 = jnp.float32
BF16 = jnp.bfloat16

D_MODEL = 1024
HEAD_DIM = 64
PAGE_SIZE = 128
H_A, G_A = 8, 2
R_A = H_A // G_A
L_CMP, L_SEL, N_SEL, WINDOW = 32, 64, 16, 512
H_B, G_B = 8, 2
R_B = H_B // G_B
H_C, Q_LORA, KV_LORA, D_NOPE, D_ROPE, D_V = 16, 512, 256, 64, 32, 64
ROPE_THETA = 10000.0
MLA_SCALE = (D_NOPE + D_ROPE) ** -0.5
N_BUCKETS, MAX_DISTANCE = 32, 128
TOP_K = 4
SWIGLU_LIMIT, SWIGLU_ALPHA = 7.0, 1.702
RMS_EPS = 1e-6
KV_W = 2 * G_A * HEAD_DIM

NEG_BIG = -1e30
SEL_NEG = -30000.0
TQ = 128
TK = 512
VMEM_LIMIT = 56 * 1024 * 1024


def _bf(x):
    return x.astype(BF16)


def _dot(a, b):
    return jnp.dot(_bf(a), _bf(b), preferred_element_type=F32)


def _dot_nt(a, b):
    return lax.dot_general(_bf(a), _bf(b), (((1,), (1,)), ((), ())), preferred_element_type=F32)


def _split2(x):
    hi = _bf(x)
    return hi, _bf(x - hi.astype(F32))


def _split3(x):
    hi = _bf(x)
    r = x - hi.astype(F32)
    mid = _bf(r)
    return hi, mid, _bf(r - mid.astype(F32))


def _dot3(a, b):
    ah, al = _split2(a)
    bh, bl = _split2(b)
    d = functools.partial(jnp.dot, preferred_element_type=F32)
    return d(ah, bh) + d(ah, bl) + d(al, bh)


def _dot3_nt(a, b):
    ah, al = _split2(a)
    bh, bl = _split2(b)
    d = functools.partial(lax.dot_general, dimension_numbers=(((1,), (1,)), ((), ())),
                          preferred_element_type=F32)
    return d(ah, bh) + d(ah, bl) + d(al, bh)


def _rms(x, g):
    return x * lax.rsqrt(jnp.mean(x * x, axis=-1, keepdims=True) + RMS_EPS) * g


def _log_sigmoid(z):
    return jnp.minimum(z, 0.0) - jnp.log1p(jnp.exp(-jnp.abs(z)))


def _t5_bucket(dist):
    n = jnp.maximum(dist, 0)
    max_exact = N_BUCKETS // 2
    nf = jnp.maximum(n, 1).astype(F32)
    large = max_exact + (jnp.log(nf / max_exact) / math.log(MAX_DISTANCE / max_exact)
                         * (N_BUCKETS - max_exact)).astype(jnp.int32)
    return jnp.where(n < max_exact, n, jnp.minimum(large, N_BUCKETS - 1))


def _params(sem, vmem=VMEM_LIMIT):
    return pltpu.CompilerParams(dimension_semantics=sem, vmem_limit_bytes=vmem)


class Dims:
    def __init__(self, B, T, DB, DT):
        self.B, self.T, self.DB, self.DT = B, T, DB, DT
        self.Mp, self.Ms = B * T, DB * DT
        self.M = self.Mp + self.Ms
        self.tm = math.gcd(512, self.Ms)
        assert T % self.tm == 0 and T % TK == 0 and self.tm % 8 == 0
        self.n_p = self.Mp // self.tm
        self.n_t = self.M // self.tm
        self.tiles_per_b = T // self.tm

    def mod_specs(self):
        tpb, nb, n_p, tm = self.tiles_per_b, self.B, self.n_p, self.tm
        sp = pl.BlockSpec((1, 1, D_MODEL), lambda i, *_: (jnp.minimum(i // tpb, nb - 1), 0, 0))
        ss = pl.BlockSpec((tm, D_MODEL), lambda i, *_: (jnp.maximum(i - n_p, 0), 0))
        return sp, ss


def _pick_mod(i, n_p, mp_ref, ms_ref):
    return jnp.where(i < n_p, mp_ref[0], ms_ref[...])


def _ada_kernel(c_ref, w_ref, b_ref, o_ref):
    c = c_ref[...]
    o_ref[0] = _dot3(c * jax.nn.sigmoid(c), w_ref[0]) + b_ref[0]


def ada_call(c_all, w_ada, b_ada):
    depth, _, n6 = w_ada.shape
    n_c = c_all.shape[0]
    tn = 1536
    return pl.pallas_call(
        _ada_kernel,
        grid=(depth, n6 // tn),
        in_specs=[pl.BlockSpec((n_c, D_MODEL), lambda l, j: (0, 0)),
                  pl.BlockSpec((1, D_MODEL, tn), lambda l, j: (l, 0, j)),
                  pl.BlockSpec((1, 1, tn), lambda l, j: (l, 0, j))],
        out_specs=pl.BlockSpec((1, n_c, tn), lambda l, j: (l, 0, j)),
        out_shape=jax.ShapeDtypeStruct((depth, n_c, n6), F32),
        compiler_params=_params(("arbitrary", "arbitrary")),
        name="ada_modulation",
    )(c_all, w_ada, b_ada.reshape(depth, 1, n6))


def _t5_strip_kernel(tab_ref, o_ref):
    p = pl.program_id(0)
    i = lax.broadcasted_iota(jnp.int32, (TQ, TK), 0)
    j = lax.broadcasted_iota(jnp.int32, (TQ, TK), 1)
    off = jnp.where(p < 4, p * TQ, jnp.where(p == 4, TK, 4 * TK))
    bucket = _t5_bucket(off + i - j)
    for h in range(H_A):
        acc = jnp.zeros((TQ, TK), F32)
        for b in range(N_BUCKETS):
            acc = jnp.where(bucket == b, tab_ref[b, h], acc)
        o_ref[h, 0] = acc


def t5_strips_call(t5_table):
    return pl.pallas_call(
        _t5_strip_kernel,
        grid=(6,),
        in_specs=[pl.BlockSpec(memory_space=pltpu.SMEM)],
        out_specs=pl.BlockSpec((H_A, 1, TQ, TK), lambda p: (0, p, 0, 0)),
        out_shape=jax.ShapeDtypeStruct((H_A, 6, TQ, TK), F32),
        compiler_params=_params(("arbitrary",)),
        name="t5_strips",
    )(t5_table)


def _proj_even_kernel(x_ref, g_ref, shp_ref, shs_ref, scp_ref, scs_ref, w_ref, bf_ref,
                      cmp_ref, sel_ref, win_ref, kvb_ref, gf_ref, logf_ref, q_ref, kvh_ref, *, n_p):
    i = pl.program_id(0)
    shift = _pick_mod(i, n_p, shp_ref, shs_ref)
    scale = _pick_mod(i, n_p, scp_ref, scs_ref)
    hb = _bf(_rms(x_ref[...], g_ref[...]) * (1.0 + scale) + shift)
    q = jnp.dot(hb, w_ref[:, 0:1024], preferred_element_type=F32) * (HEAD_DIM ** -0.5)
    for h in range(H_A + H_B):
        q_ref[h] = _bf(q[:, h * HEAD_DIM:(h + 1) * HEAD_DIM])
    kv = jnp.dot(hb, w_ref[:, 1024:2048], preferred_element_type=F32)
    cmp_ref[...] = kv[:, 0:256]
    sel_ref[...] = kv[:, 256:512]
    win_ref[...] = kv[:, 512:768]
    kvb_ref[...] = kv[:, 768:1024]
    for c in range(12):
        kvh_ref[c] = _bf(kv[:, 256 + c * HEAD_DIM:256 + (c + 1) * HEAD_DIM])
    gfr = jnp.dot(hb, w_ref[:, 2048:2176], preferred_element_type=F32)
    lane = lax.broadcasted_iota(jnp.int32, gfr.shape, 1)
    lsg = _log_sigmoid(gfr + bf_ref[...])
    gf_ref[...] = jnp.where(lane < H_B, lsg, jax.nn.sigmoid(gfr))
    logf_ref[...] = lsg[:, 0:H_B]


def proj_even_call(dm, x, g, shift, scale, w_perm, bf_row):
    tm, M = dm.tm, dm.M
    sp, ss = dm.mod_specs()
    row = lambda w: pl.BlockSpec((tm, w), lambda i: (i, 0))
    outs = [jax.ShapeDtypeStruct((M, KV_W), F32)] * 4 + [
        jax.ShapeDtypeStruct((M, 128), F32), jax.ShapeDtypeStruct((M, H_B), F32),
        jax.ShapeDtypeStruct((H_A + H_B, M, HEAD_DIM), BF16),
        jax.ShapeDtypeStruct((12, M, HEAD_DIM), BF16)]
    return pl.pallas_call(
        functools.partial(_proj_even_kernel, n_p=dm.n_p),
        grid=(dm.n_t,),
        in_specs=[row(D_MODEL), pl.BlockSpec((1, D_MODEL), lambda i: (0, 0)), sp, ss, sp, ss,
                  pl.BlockSpec((D_MODEL, 2176), lambda i: (0, 0)),
                  pl.BlockSpec((1, 128), lambda i: (0, 0))],
        out_specs=[row(KV_W)] * 4 + [row(128), row(H_B),
                   pl.BlockSpec((H_A + H_B, tm, HEAD_DIM), lambda i: (0, i, 0)),
                   pl.BlockSpec((12, tm, HEAD_DIM), lambda i: (0, i, 0))],
        out_shape=outs,
        compiler_params=_params(("arbitrary",)),
        name="proj_even",
    )(x, g, shift[0], shift[1], scale[0], scale[1], w_perm, bf_row)


def _compress_kernel(x_ref, pe_ref, w_ref, o_ref, oh_ref):
    y = _dot(x_ref[...] + pe_ref[...], w_ref[...])
    o_ref[...] = y
    for c in range(4):
        oh_ref[c] = _bf(y[:, c * HEAD_DIM:(c + 1) * HEAD_DIM])


def compress_call(rows_blk, pe_flat, w_big):
    n_blk, kdim = rows_blk.shape
    tb = math.gcd(n_blk, 256)
    return pl.pallas_call(
        _compress_kernel,
        grid=(n_blk // tb,),
        in_specs=[pl.BlockSpec((tb, kdim), lambda i: (i, 0)),
                  pl.BlockSpec((1, kdim), lambda i: (0, 0)),
                  pl.BlockSpec((kdim, KV_W), lambda i: (0, 0))],
        out_specs=[pl.BlockSpec((tb, KV_W), lambda i: (i, 0)),
                   pl.BlockSpec((4, tb, HEAD_DIM), lambda i: (0, i, 0))],
        out_shape=[jax.ShapeDtypeStruct((n_blk, KV_W), F32),
                   jax.ShapeDtypeStruct((4, n_blk, HEAD_DIM), BF16)],
        compiler_params=_params(("arbitrary",)),
        name="compress",
    )(rows_blk, pe_flat, w_big)


def _cumsum_kernel(x_ref, o_ref, carry_ref):
    j = pl.program_id(0)

    @pl.when(j == 0)
    def _():
        carry_ref[...] = jnp.zeros_like(carry_ref)

    n = x_ref.shape[1]
    u = lax.broadcasted_iota(jnp.int32, (n, n), 0)
    s = lax.broadcasted_iota(jnp.int32, (n, n), 1)
    tri = (u <= s).astype(BF16)
    hi, mid, lo = _split3(x_ref[...])
    d = functools.partial(jnp.dot, preferred_element_type=F32)
    c = d(hi, tri) + d(mid, tri) + d(lo, tri) + carry_ref[...]
    o_ref[...] = c
    carry_ref[...] = c[:, n - 1:n]


def cumsum_call(x):
    rows, n = x.shape
    tn = math.gcd(n, 512)
    return pl.pallas_call(
        _cumsum_kernel,
        grid=(n // tn,),
        in_specs=[pl.BlockSpec((rows, tn), lambda j: (0, j))],
        out_specs=pl.BlockSpec((rows, tn), lambda j: (0, j)),
        out_shape=jax.ShapeDtypeStruct((rows, n), F32),
        scratch_shapes=[pltpu.VMEM((rows, 1), F32)],
        compiler_params=_params(("arbitrary",)),
        name="cumsum_logf",
    )(x)


def _cmp_select_kernel(tab_ref, q_ref, ck_ref, cv_ref, o_ref, sel_ref, *, n_sel_blocks, k_sel, pos0, pos_mod):
    g = pl.program_id(0) % G_A
    qi = pl.program_id(1)
    tq = q_ref.shape[1]
    nc = ck_ref.shape[1]
    row = lax.broadcasted_iota(jnp.int32, (tq, nc), 0)
    if pos_mod is None:
        tpos = pos0 + qi * tq + row
    else:
        tpos = pos0 + row % pos_mod
    c_end = (lax.broadcasted_iota(jnp.int32, (tq, nc), 1) + 1) * L_CMP - 1
    vis = c_end <= tpos
    bucket = _t5_bucket(tpos - c_end)
    q = q_ref[...]
    lg = _dot_nt(q.reshape(R_A * tq, HEAD_DIM), ck_ref[0]).reshape(R_A, tq, nc)
    imp = jnp.zeros((tq, nc), F32)
    for r in range(R_A):
        bias = jnp.zeros((tq, nc), F32)
        for b in range(N_BUCKETS):
            bias = jnp.where(bucket == b, tab_ref[b, g * R_A + r], bias)
        lr = jnp.where(vis, lg[r] + bias, NEG_BIG)
        m = jnp.max(lr, axis=-1, keepdims=True)
        e = jnp.where(vis, jnp.exp(lr - m), 0.0)
        s = jnp.sum(e, axis=-1, keepdims=True)
        p = e / jnp.where(s > 0, s, 1.0)
        o_ref[r] = _dot(p, cv_ref[0])
        imp = imp + p
    nsp = sel_ref.shape[2]
    cpb = L_SEL // L_CMP
    pr = lax.broadcasted_iota(jnp.int32, (nc, nsp), 0) // cpb
    pc = lax.broadcasted_iota(jnp.int32, (nc, nsp), 1)
    pair = (pr == pc).astype(BF16)
    hi, mid, lo = _split3(imp)
    d = functools.partial(jnp.dot, preferred_element_type=F32)
    imp_s = d(hi, pair) + d(mid, pair) + d(lo, pair)
    blk = lax.broadcasted_iota(jnp.int32, (tq, nsp), 1)
    cur = tpos[:, 0:1] // L_SEL
    valid = (blk <= cur) & (blk < n_sel_blocks)
    forced = (blk == 0) | (blk == cur) | (blk == cur - 1)
    score = jnp.where(forced & valid, jnp.inf, jnp.where(valid, imp_s, -jnp.inf))
    chosen = jnp.zeros((tq, nsp), jnp.bool_)
    for _ in range(k_sel):
        m = jnp.max(score, axis=-1, keepdims=True)
        idx = jnp.min(jnp.where(score == m, blk, nsp), axis=-1, keepdims=True)
        hit = blk == idx
        chosen = chosen | hit
        score = jnp.where(hit, -jnp.inf, score)
    sel_ref[0] = jnp.where(chosen, 0.0, SEL_NEG).astype(BF16)


def cmp_select_prompt_call(dm, t5_table, q_hm, cmp_hm):
    B, T, M = dm.B, dm.T, dm.M
    nq, nc = T // TQ, T // L_CMP
    ns = T // L_SEL
    nsp = max(128, ns)
    kern = functools.partial(_cmp_select_kernel, n_sel_blocks=ns, k_sel=min(N_SEL, ns), pos0=0, pos_mod=None)
    return pl.pallas_call(
        kern,
        grid=(B * G_A, nq),
        in_specs=[pl.BlockSpec(memory_space=pltpu.SMEM),
                  pl.BlockSpec((R_A, TQ, HEAD_DIM), lambda bg, qi: (bg % G_A, (bg // G_A) * nq + qi, 0)),
                  pl.BlockSpec((1, nc, HEAD_DIM), lambda bg, qi: (bg % G_A, bg // G_A, 0)),
                  pl.BlockSpec((1, nc, HEAD_DIM), lambda bg, qi: (G_A + bg % G_A, bg // G_A, 0))],
        out_specs=[pl.BlockSpec((R_A, TQ, HEAD_DIM), lambda bg, qi: (bg % G_A, (bg // G_A) * nq + qi, 0)),
                   pl.BlockSpec((1, TQ, nsp), lambda bg, qi: (bg % G_A, (bg // G_A) * nq + qi, 0))],
        out_shape=[jax.ShapeDtypeStruct((H_A, M, HEAD_DIM), F32),
                   jax.ShapeDtypeStruct((G_A, M, nsp), BF16)],
        compiler_params=_params(("arbitrary", "arbitrary")),
        name="nsa_cmp_select_prompt",
    )(t5_table, q_hm, cmp_hm, cmp_hm)


def _steps(T, window):
    qt, kt, ft = [], [], []
    for qi in range(T // TQ):
        kd = (qi * TQ) // TK
        p = (qi * TQ % TK) // TQ
        k0 = max(kd - 1, 0) if window else 0
        for ki in range(k0, kd + 1):
            strip = p if ki == kd else (4 if (ki == kd - 1 and p == 0) else 5)
            fl = (1 if ki == k0 else 0) | (2 if ki == kd else 0) | (4 if ki == kd else 0) | (strip << 4)
            qt.append(qi), kt.append(ki), ft.append(fl)
    return (np.asarray(qt, np.int32), np.asarray(kt, np.int32), np.asarray(ft, np.int32))


def _flash_kernel(qt_ref, kt_ref, ft_ref, *refs, mode, rows_per_tok):
    if mode == "mla":
        q_ref, qpe_ref, k_ref, kpe_ref, o_ref, m_s, l_s, acc_s = refs
    elif mode == "sel":
        q_ref, k_ref, v_ref, selneg_ref, strip_ref, o_ref, m_s, l_s, acc_s = refs
    elif mode == "win":
        q_ref, k_ref, v_ref, strip_ref, o_ref, m_s, l_s, acc_s = refs
    else:
        q_ref, k_ref, v_ref, o_ref, m_s, l_s, acc_s = refs
    s = pl.program_id(1)
    qi, ki, fl = qt_ref[s], kt_ref[s], ft_ref[s]
    rows = m_s.shape[0]
    tk = k_ref.shape[-2]

    @pl.when((fl & 1) != 0)
    def _init():
        m_s[...] = jnp.full_like(m_s, NEG_BIG)
        l_s[...] = jnp.zeros_like(l_s)
        acc_s[...] = jnp.zeros_like(acc_s)

    def step(masked):
        if mode == "mla":
            kb = k_ref[...]
            lg = _dot_nt(q_ref[...], kb) + _dot_nt(qpe_ref[...], kpe_ref[...])
            vb = kb
        else:
            q = q_ref[...]
            lg = _dot_nt(q.reshape(rows, q.shape[-1]), k_ref[0])
            vb = v_ref[0]
        if mode == "sel":
            nsp = selneg_ref.shape[2]
            blk = lax.broadcasted_iota(jnp.int32, (nsp, tk), 0)
            key = ki * tk + lax.broadcasted_iota(jnp.int32, (nsp, tk), 1)
            onehot = (blk == key // L_SEL).astype(BF16)
            sb = jnp.dot(selneg_ref[0], onehot, preferred_element_type=F32)
            lg = (lg.reshape(R_A, TQ, tk) + sb[None]).reshape(rows, tk)
        if mode in ("sel", "win"):
            lg = lg + strip_ref[:, 0].reshape(rows, tk)
        if masked:
            r = lax.broadcasted_iota(jnp.int32, (rows, tk), 0)
            tok = r // rows_per_tok if rows_per_tok > 1 else r % TQ
            dist = (qi * TQ + tok) - (ki * tk + lax.broadcasted_iota(jnp.int32, (rows, tk), 1))
            ok = dist >= 0
            if mode == "win":
                ok = ok & (dist < WINDOW)
            lg = jnp.where(ok, lg, NEG_BIG)
        m_prev = m_s[...]
        m_new = jnp.maximum(m_prev, jnp.max(lg, axis=-1, keepdims=True))
        alpha = jnp.exp(m_prev - m_new)
        p = jnp.exp(lg - m_new)
        l_s[...] = alpha * l_s[...] + jnp.sum(p, axis=-1, keepdims=True)
        acc_s[...] = alpha * acc_s[...] + _dot(p, vb)
        m_s[...] = m_new

    if mode == "win":
        step(True)
    else:
        @pl.when((fl & 4) != 0)
        def _diag():
            step(True)

        @pl.when((fl & 4) == 0)
        def _inner():
            step(False)

    @pl.when((fl & 2) != 0)
    def _fin():
        o = acc_s[...] / l_s[...]
        o_ref[...] = o.reshape(o_ref.shape).astype(o_ref.dtype)


def _flash_scratch(rows, dv):
    return [pltpu.VMEM((rows, 1), F32), pltpu.VMEM((rows, 1), F32), pltpu.VMEM((rows, dv), F32)]


def flash_gqa_call(dm, mode, q, q_head0, k, k_idx0, v, v_idx0, n_heads_out, selneg=None, strips=None):
    B, T, M = dm.B, dm.T, dm.M
    R, G = R_A, G_A
    nq, nk = T // TQ, T // TK
    qt, kt, ft = _steps(T, window=(mode == "win"))
    dq, dk, dv = q.shape[-1], k.shape[-1], v.shape[-1]
    qh0 = q_head0 // R
    in_specs = [
        pl.BlockSpec((R, TQ, dq), lambda bg, s, qt, kt, ft: (qh0 + bg % G, (bg // G) * nq + qt[s], 0)),
        pl.BlockSpec((1, TK, dk), lambda bg, s, qt, kt, ft: (k_idx0 + bg % G, (bg // G) * nk + kt[s], 0)),
        pl.BlockSpec((1, TK, dv), lambda bg, s, qt, kt, ft: (v_idx0 + bg % G, (bg // G) * nk + kt[s], 0)),
    ]
    args = [q, k, v]
    if mode == "sel":
        nsp = selneg.shape[-1]
        in_specs.append(pl.BlockSpec((1, TQ, nsp), lambda bg, s, qt, kt, ft: (bg % G, (bg // G) * nq + qt[s], 0)))
        args.append(selneg)
    if mode in ("sel", "win"):
        in_specs.append(pl.BlockSpec((R, 1, TQ, TK), lambda bg, s, qt, kt, ft: (bg % G, ft[s] >> 4, 0, 0)))
        args.append(strips)
    grid_spec = pltpu.PrefetchScalarGridSpec(
        num_scalar_prefetch=3,
        grid=(B * G, len(qt)),
        in_specs=in_specs,
        out_specs=pl.BlockSpec((R, TQ, dv), lambda bg, s, qt, kt, ft: (bg % G, (bg // G) * nq + qt[s], 0)),
        scratch_shapes=_flash_scratch(R * TQ, dv),
    )
    return pl.pallas_call(
        functools.partial(_flash_kernel, mode=mode, rows_per_tok=1),
        grid_spec=grid_spec,
        out_shape=jax.ShapeDtypeStruct((n_heads_out, M, dv), F32),
        compiler_params=_params(("arbitrary", "arbitrary")),
        name="flash_" + mode,
    )(jnp.asarray(qt), jnp.asarray(kt), jnp.asarray(ft), *args)


def flash_mla_call(dm, q_lat, q_pe, ckv, kpe):
    B, T, M = dm.B, dm.T, dm.M
    nq, nk = T // TQ, T // TK
    qt, kt, ft = _steps(T, window=False)
    rows = TQ * H_C
    grid_spec = pltpu.PrefetchScalarGridSpec(
        num_scalar_prefetch=3,
        grid=(B, len(qt)),
        in_specs=[
            pl.BlockSpec((rows, KV_LORA), lambda b, s, qt, kt, ft: (b * nq + qt[s], 0)),
            pl.BlockSpec((rows, D_ROPE), lambda b, s, qt, kt, ft: (b * nq + qt[s], 0)),
            pl.BlockSpec((TK, KV_LORA), lambda b, s, qt, kt, ft: (b * nk + kt[s], 0)),
            pl.BlockSpec((TK, D_ROPE), lambda b, s, qt, kt, ft: (b * nk + kt[s], 0)),
        ],
        out_specs=pl.BlockSpec((rows, KV_LORA), lambda b, s, qt, kt, ft: (b * nq + qt[s], 0)),
        scratch_shapes=_flash_scratch(rows, KV_LORA),
    )
    return pl.pallas_call(
        functools.partial(_flash_kernel, mode="mla", rows_per_tok=H_C),
        grid_spec=grid_spec,
        out_shape=jax.ShapeDtypeStruct((M * H_C, KV_LORA), BF16),
        compiler_params=_params(("arbitrary", "arbitrary")),
        name="flash_mla",
    )(jnp.asarray(qt), jnp.asarray(kt), jnp.asarray(ft), q_lat, q_pe, ckv, kpe)


def _out_even_kernel(x_ref, gp_ref, gs_ref, gf_ref, oc_ref, os_ref, ow_ref, ob_ref, w_ref, o_ref, *, n_p):
    i = pl.program_id(0)
    gf = gf_ref[...]
    acc = jnp.zeros(x_ref.shape, F32)
    for h in range(H_A):
        a = (gf[:, H_B + h:H_B + h + 1] * oc_ref[h]
             + gf[:, H_B + H_A + h:H_B + H_A + h + 1] * os_ref[h]
             + gf[:, H_B + 2 * H_A + h:H_B + 2 * H_A + h + 1] * ow_ref[h])
        acc = acc + _dot(a, w_ref[h])
    for h in range(H_B):
        acc = acc + _dot(ob_ref[h], w_ref[H_A + h])
    o_ref[...] = x_ref[...] + _pick_mod(i, n_p, gp_ref, gs_ref) * acc


def out_even_call(dm, x, gate, gf, o_cmp, o_sel, o_win, o_fox, w_out_h):
    tm, M = dm.tm, dm.M
    sp, ss = dm.mod_specs()
    hm = pl.BlockSpec((H_A, tm, HEAD_DIM), lambda i: (0, i, 0))
    return pl.pallas_call(
        functools.partial(_out_even_kernel, n_p=dm.n_p),
        grid=(dm.n_t,),
        in_specs=[pl.BlockSpec((tm, D_MODEL), lambda i: (i, 0)), sp, ss,
                  pl.BlockSpec((tm, 128), lambda i: (i, 0)), hm, hm, hm, hm,
                  pl.BlockSpec((H_A + H_B, HEAD_DIM, D_MODEL), lambda i: (0, 0, 0))],
        out_specs=pl.BlockSpec((tm, D_MODEL), lambda i: (i, 0)),
        out_shape=jax.ShapeDtypeStruct((M, D_MODEL), F32),
        compiler_params=_params(("arbitrary",)),
        name="out_even",
    )(x, gate[0], gate[1], gf, o_cmp, o_sel, o_win, o_fox, w_out_h)


def _proj_odd_kernel(x_ref, g_ref, shp_ref, shs_ref, scp_ref, scs_ref, w_ref, gq_ref, gkv_ref, cos_ref, sin_ref,
                     cqn_ref, ckv_ref, ckvb_ref, kpe_ref, kpeb_ref, *, n_p):
    i = pl.program_id(0)
    shift = _pick_mod(i, n_p, shp_ref, shs_ref)
    scale = _pick_mod(i, n_p, scp_ref, scs_ref)
    hb = _bf(_rms(x_ref[...], g_ref[...]) * (1.0 + scale) + shift)
    cq = jnp.dot(hb, w_ref[:, 0:Q_LORA], preferred_element_type=F32)
    cqn_ref[...] = _bf(_rms(cq, gq_ref[...]))
    ckv = _rms(jnp.dot(hb, w_ref[:, Q_LORA:Q_LORA + KV_LORA], preferred_element_type=F32), gkv_ref[...])
    ckv_ref[...] = ckv
    ckvb_ref[...] = _bf(ckv)
    pe = jnp.dot(hb, w_ref[:, Q_LORA + KV_LORA:Q_LORA + KV_LORA + 128], preferred_element_type=F32)
    kpe = pe[:, 0:D_ROPE] * cos_ref[...] + pe[:, D_ROPE:2 * D_ROPE] * sin_ref[...]
    kpe_ref[...] = kpe
    kpeb_ref[...] = _bf(kpe)


def proj_odd_call(dm, x, g, shift, scale, w_perm, g_q, g_kv, cos32, sin32):
    tm, M = dm.tm, dm.M
    sp, ss = dm.mod_specs()
    row = lambda w: pl.BlockSpec((tm, w), lambda i: (i, 0))
    const = lambda r, c: pl.BlockSpec((r, c), lambda i: (0, 0))
    return pl.pallas_call(
        functools.partial(_proj_odd_kernel, n_p=dm.n_p),
        grid=(dm.n_t,),
        in_specs=[row(D_MODEL), const(1, D_MODEL), sp, ss, sp, ss, const(D_MODEL, 896),
                  const(1, Q_LORA), const(1, KV_LORA), row(D_ROPE), row(D_ROPE)],
        out_specs=[row(Q_LORA), row(KV_LORA), row(KV_LORA), row(D_ROPE), row(D_ROPE)],
        out_shape=[jax.ShapeDtypeStruct((M, Q_LORA), BF16), jax.ShapeDtypeStruct((M, KV_LORA), F32),
                   jax.ShapeDtypeStruct((M, KV_LORA), BF16), jax.ShapeDtypeStruct((M, D_ROPE), F32),
                   jax.ShapeDtypeStruct((M, D_ROPE), BF16)],
        compiler_params=_params(("arbitrary",)),
        name="proj_odd",
    )(x, g, shift[0], shift[1], scale[0], scale[1], w_perm, g_q, g_kv, cos32, sin32)


def _bmm_nt_kernel(a_ref, b_ref, o_ref):
    o_ref[0] = _dot3_nt(a_ref[0], b_ref[0])


def _bmm_kernel(a_ref, b_ref, o_ref):
    o_ref[0] = _dot3(a_ref[0], b_ref[0])


def bmm_call(a, b, nt, name):
    H, m, _ = a.shape
    n = b.shape[1] if nt else b.shape[2]
    return pl.pallas_call(
        _bmm_nt_kernel if nt else _bmm_kernel,
        grid=(H,),
        in_specs=[pl.BlockSpec((1,) + a.shape[1:], lambda h: (h, 0, 0)),
                  pl.BlockSpec((1,) + b.shape[1:], lambda h: (h, 0, 0))],
        out_specs=pl.BlockSpec((1, m, n), lambda h: (h, 0, 0)),
        out_shape=jax.ShapeDtypeStruct((H, m, n), F32),
        compiler_params=_params(("arbitrary",)),
        name=name,
    )(a, b)


def _qlat_kernel(a_ref, w_ref, o_ref):
    o_ref[...] = _bf(jnp.dot(a_ref[...], w_ref[...], preferred_element_type=F32) * MLA_SCALE)


def _qpe_kernel(a_ref, w_ref, cos_ref, sin_ref, o_ref):
    y = jnp.dot(a_ref[...], w_ref[...], preferred_element_type=F32)
    n = o_ref.shape[1]
    o_ref[...] = _bf((y[:, 0:n] * cos_ref[...] + y[:, n:2 * n] * sin_ref[...]) * MLA_SCALE)


def q_mla_call(dm, cqn, w_qabs, w_qpe, cos_h, sin_h):
    tm, M = dm.tm, dm.M
    n_lat = H_C * KV_LORA
    n_pe = H_C * D_ROPE
    tn = 1024
    q_lat = pl.pallas_call(
        _qlat_kernel,
        grid=(dm.n_t, n_lat // tn),
        in_specs=[pl.BlockSpec((tm, Q_LORA), lambda i, j: (i, 0)),
                  pl.BlockSpec((Q_LORA, tn), lambda i, j: (0, j))],
        out_specs=pl.BlockSpec((tm, tn), lambda i, j: (i, j)),
        out_shape=jax.ShapeDtypeStruct((M, n_lat), BF16),
        compiler_params=_params(("arbitrary", "arbitrary")),
        name="q_lat",
    )(cqn, w_qabs)
    q_pe = pl.pallas_call(
        _qpe_kernel,
        grid=(dm.n_t,),
        in_specs=[pl.BlockSpec((tm, Q_LORA), lambda i: (i, 0)),
                  pl.BlockSpec((Q_LORA, 2 * n_pe), lambda i: (0, 0)),
                  pl.BlockSpec((tm, n_pe), lambda i: (i, 0)),
                  pl.BlockSpec((tm, n_pe), lambda i: (i, 0))],
        out_specs=pl.BlockSpec((tm, n_pe), lambda i: (i, 0)),
        out_shape=jax.ShapeDtypeStruct((M, n_pe), BF16),
        compiler_params=_params(("arbitrary",)),
        name="q_pe",
    )(cqn, w_qpe, cos_h, sin_h)
    return q_lat, q_pe


def _out_odd_kernel(x_ref, gp_ref, gs_ref, o_ref_in, w_ref, o_ref, *, n_p):
    i = pl.program_id(0)
    y = jnp.dot(o_ref_in[...], w_ref[...], preferred_element_type=F32)
    o_ref[...] = x_ref[...] + _pick_mod(i, n_p, gp_ref, gs_ref) * y


def out_odd_call(dm, x, gate, o_lat, w_comb):
    tm, M = dm.tm, dm.M
    sp, ss = dm.mod_specs()
    kdim = H_C * KV_LORA
    return pl.pallas_call(
        functools.partial(_out_odd_kernel, n_p=dm.n_p),
        grid=(dm.n_t,),
        in_specs=[pl.BlockSpec((tm, D_MODEL), lambda i: (i, 0)), sp, ss,
                  pl.BlockSpec((tm, kdim), lambda i: (i, 0)),
                  pl.BlockSpec((kdim, D_MODEL), lambda i: (0, 0))],
        out_specs=pl.BlockSpec((tm, D_MODEL), lambda i: (i, 0)),
        out_shape=jax.ShapeDtypeStruct((M, D_MODEL), F32),
        compiler_params=_params(("arbitrary",)),
        name="out_odd",
    )(x, gate[0], gate[1], o_lat, w_comb)


def _moe_pre_kernel(x_ref, g_ref, shp_ref, shs_ref, scp_ref, scs_ref, wr_ref, br_ref, h_ref, gates_ref, *, n_p):
    i = pl.program_id(0)
    shift = _pick_mod(i, n_p, shp_ref, shs_ref)
    scale = _pick_mod(i, n_p, scp_ref, scs_ref)
    h = _rms(x_ref[...], g_ref[...]) * (1.0 + scale) + shift
    h_ref[...] = _bf(h)
    logits = _dot3(h, wr_ref[...]) + br_ref[...]
    ne = logits.shape[1]
    lane = lax.broadcasted_iota(jnp.int32, logits.shape, 1)
    work = logits
    un = jnp.zeros_like(logits)
    den = jnp.zeros((logits.shape[0], 1), F32)
    m0 = None
    for k in range(TOP_K):
        m = jnp.max(work, axis=-1, keepdims=True)
        idx = jnp.min(jnp.where(work == m, lane, ne), axis=-1, keepdims=True)
        hit = lane == idx
        if k == 0:
            m0 = m
        e = jnp.exp(m - m0)
        un = un + jnp.where(hit, e, 0.0)
        den = den + e
        work = jnp.where(hit, -jnp.inf, work)
    gates_ref[...] = un / den


def moe_pre_call(dm, x, g, shift, scale, w_router, b_router):
    tm, M = dm.tm, dm.M
    ne = w_router.shape[1]
    sp, ss = dm.mod_specs()
    return pl.pallas_call(
        functools.partial(_moe_pre_kernel, n_p=dm.n_p),
        grid=(dm.n_t,),
        in_specs=[pl.BlockSpec((tm, D_MODEL), lambda i: (i, 0)), pl.BlockSpec((1, D_MODEL), lambda i: (0, 0)),
                  sp, ss, sp, ss,
                  pl.BlockSpec((D_MODEL, ne), lambda i: (0, 0)), pl.BlockSpec((1, ne), lambda i: (0, 0))],
        out_specs=[pl.BlockSpec((tm, D_MODEL), lambda i: (i, 0)), pl.BlockSpec((tm, ne), lambda i: (i, 0))],
        out_shape=[jax.ShapeDtypeStruct((M, D_MODEL), BF16), jax.ShapeDtypeStruct((M, ne), F32)],
        compiler_params=_params(("arbitrary",)),
        name="moe_router",
    )(x, g, shift[0], shift[1], scale[0], scale[1], w_router, b_router)


def _moe_dense_kernel(h_ref, gates_ref, wgu_ref, bgu_ref, wdn_ref, bdn_ref, x_ref, gp_ref, gs_ref,
                      o_ref, acc_ref, *, n_p, d_ff):
    i = pl.program_id(0)
    e = pl.program_id(1)

    @pl.when(e == 0)
    def _():
        acc_ref[...] = jnp.zeros_like(acc_ref)

    gu = jnp.dot(h_ref[...], wgu_ref[0], preferred_element_type=F32) + bgu_ref[0]
    gate = jnp.minimum(gu[:, :d_ff], SWIGLU_LIMIT)
    up = jnp.clip(gu[:, d_ff:], -SWIGLU_LIMIT, SWIGLU_LIMIT)
    act = (up + 1.0) * gate * jax.nn.sigmoid(SWIGLU_ALPHA * gate)
    y = jnp.dot(_bf(act), wdn_ref[0], preferred_element_type=F32) + bdn_ref[0]
    gates = gates_ref[...]
    lane = lax.broadcasted_iota(jnp.int32, gates.shape, 1)
    g_e = jnp.sum(jnp.where(lane == e, gates, 0.0), axis=-1, keepdims=True)
    acc_ref[...] += g_e * y

    @pl.when(e == pl.num_programs(1) - 1)
    def _():
        o_ref[...] = x_ref[...] + _pick_mod(i, n_p, gp_ref, gs_ref) * acc_ref[...]


def moe_dense_call(dm, h, gates, w_gu, b_gu, w_dn, b_dn, x, gate):
    tm, M = dm.tm, dm.M
    ne, _, n2 = w_gu.shape
    d_ff = n2 // 2
    sp, ss = dm.mod_specs()
    return pl.pallas_call(
        functools.partial(_moe_dense_kernel, n_p=dm.n_p, d_ff=d_ff),
        grid=(dm.n_t, ne),
        in_specs=[pl.BlockSpec((tm, D_MODEL), lambda i, e: (i, 0)),
                  pl.BlockSpec((tm, ne), lambda i, e: (i, 0)),
                  pl.BlockSpec((1, D_MODEL, n2), lambda i, e: (e, 0, 0)),
                  pl.BlockSpec((1, 1, n2), lambda i, e: (e, 0, 0)),
                  pl.BlockSpec((1, d_ff, D_MODEL), lambda i, e: (e, 0, 0)),
                  pl.BlockSpec((1, 1, D_MODEL), lambda i, e: (e, 0, 0)),
                  pl.BlockSpec((tm, D_MODEL), lambda i, e: (i, 0)), sp, ss],
        out_specs=pl.BlockSpec((tm, D_MODEL), lambda i, e: (i, 0)),
        out_shape=jax.ShapeDtypeStruct((M, D_MODEL), F32),
        scratch_shapes=[pltpu.VMEM((tm, D_MODEL), F32)],
        compiler_params=_params(("arbitrary", "arbitrary")),
        name="moe_experts",
    )(h, gates, w_gu, b_gu.reshape(ne, 1, n2), w_dn, b_dn.reshape(ne, 1, D_MODEL), x, gate[0], gate[1])


def _final_norm_kernel(x_ref, g_ref, o_ref):
    o_ref[...] = _rms(x_ref[...], g_ref[...])


def final_norm_call(dm, x, g):
    tm = dm.tm
    return pl.pallas_call(
        _final_norm_kernel,
        grid=(dm.n_t,),
        in_specs=[pl.BlockSpec((tm, D_MODEL), lambda i: (i, 0)), pl.BlockSpec((1, D_MODEL), lambda i: (0, 0))],
        out_specs=pl.BlockSpec((tm, D_MODEL), lambda i: (i, 0)),
        out_shape=jax.ShapeDtypeStruct((dm.M, D_MODEL), F32),
        compiler_params=_params(("arbitrary",)),
        name="final_norm",
    )(x, g)


def _prep_even(w_in, b_f, w_cmp, pe_cmp, w_out):
    a_q, a_kv, a_g, b_q = H_A * HEAD_DIM, KV_W, 3 * H_A, H_B * HEAD_DIM
    o = np.cumsum([0, a_q, a_kv, a_kv, a_kv, a_g, b_q, KV_W, H_B])
    q_a, kv_cmp, kv_sel, kv_win, g_a, q_b, kv_b, f_b = [w_in[:, o[i]:o[i + 1]] for i in range(8)]
    pad = jnp.zeros((D_MODEL, 128 - H_B - a_g), w_in.dtype)
    w_perm = _bf(jnp.concatenate([q_a, q_b, kv_cmp, kv_sel, kv_win, kv_b, f_b, g_a, pad], axis=1))
    bf_row = jnp.concatenate([b_f, jnp.zeros((128 - H_B,), F32)])[None]
    eye = jnp.eye(2, dtype=F32)
    w_big = _bf(jnp.einsum('kldo,kK,gG->lkgdKGo', w_cmp, eye, eye).reshape(L_CMP * KV_W, KV_W))
    pe_flat = jnp.broadcast_to(jnp.transpose(pe_cmp, (1, 0, 2))[:, :, None, :],
                               (L_CMP, 2, G_A, HEAD_DIM)).reshape(1, L_CMP * KV_W)
    w_out_h = _bf(w_out.reshape(H_A + H_B, HEAD_DIM, D_MODEL))
    return w_perm, bf_row, w_big, pe_flat, w_out_h


def _rot_cols(w):
    half = D_ROPE // 2
    return jnp.concatenate([-w[..., half:], w[..., :half]], axis=-1)


def _prep_odd(w_in, w_qb, w_kvb, w_out):
    kpe_w = w_in[:, Q_LORA + KV_LORA:]
    pad = jnp.zeros((D_MODEL, 128 - 2 * D_ROPE), w_in.dtype)
    w_perm = _bf(jnp.concatenate([w_in[:, :Q_LORA + KV_LORA], kpe_w, _rot_cols(kpe_w), pad], axis=1))
    wq = w_qb.reshape(Q_LORA, H_C, D_NOPE + D_ROPE)
    wkv = w_kvb.reshape(KV_LORA, H_C, D_NOPE + D_V)
    nope_h = jnp.transpose(wq[..., :D_NOPE], (1, 0, 2))
    w_uk_h = jnp.transpose(wkv[..., :D_NOPE], (1, 0, 2))
    w_qabs = bmm_call(nope_h, w_uk_h, True, "w_q_absorb")
    w_qabs = _bf(jnp.transpose(w_qabs, (1, 0, 2)).reshape(Q_LORA, H_C * KV_LORA))
    pe_w = wq[..., D_NOPE:]
    w_qpe = _bf(jnp.concatenate([pe_w.reshape(Q_LORA, H_C * D_ROPE),
                                 _rot_cols(pe_w).reshape(Q_LORA, H_C * D_ROPE)], axis=1))
    w_uv_h = jnp.transpose(wkv[..., D_NOPE:], (1, 0, 2))
    w_comb = bmm_call(w_uv_h, w_out.reshape(H_C, D_V, D_MODEL), False, "w_out_absorb")
    w_comb = _bf(w_comb.reshape(H_C * KV_LORA, D_MODEL))
    return w_perm, w_qabs, w_qpe, w_comb


def _rope_tables(dm, past):
    half = D_ROPE // 2
    inv = ROPE_THETA ** (-jnp.arange(half, dtype=F32) / half)
    pos = jnp.concatenate([jnp.tile(jnp.arange(dm.T, dtype=jnp.int32), dm.B),
                           jnp.tile(past + jnp.arange(dm.DT, dtype=jnp.int32), dm.DB)])
    ang = pos.astype(F32)[:, None] * inv[None, :]
    cos32 = jnp.tile(jnp.cos(ang), (1, 2))
    sin32 = jnp.tile(jnp.sin(ang), (1, 2))
    return cos32, sin32, jnp.tile(cos32, (1, H_C)), jnp.tile(sin32, (1, H_C))


def _split3_cols(c):
    hi, mid, lo = _split3(c)
    return [hi, mid, lo]


def _fox_augment(dm, q_hm, kvh, cum):
    Mp, M = dm.Mp, dm.M
    cum = jnp.pad(cum.reshape(Mp, H_B), ((0, M - Mp), (0, 0)))
    ones = jnp.ones((M, 3), BF16)
    zeros3 = jnp.zeros((M, 3), BF16)
    qs, ks = [], []
    for g in range(G_B):
        kcols = [kvh[8 + g]]
        for r in range(R_B):
            kcols += _split3_cols(-cum[:, g * R_B + r:g * R_B + r + 1])
        kcols += [ones, jnp.zeros((M, 128 - HEAD_DIM - 3 * R_B - 3), BF16)]
        ks.append(jnp.concatenate(kcols, axis=1))
        for r in range(R_B):
            h = g * R_B + r
            qcols = [q_hm[H_A + h]] + [ones if rr == r else zeros3 for rr in range(R_B)]
            qcols += _split3_cols(cum[:, h:h + 1])
            qcols += [jnp.zeros((M, 128 - HEAD_DIM - 3 * R_B - 3), BF16)]
            qs.append(jnp.concatenate(qcols, axis=1))
    return jnp.stack(qs), jnp.stack(ks)


def _masked_softmax(lg, mask):
    lg = jnp.where(mask, lg, -jnp.inf)
    m = jnp.max(lg, axis=-1, keepdims=True)
    m = jnp.where(jnp.isfinite(m), m, 0.0)
    e = jnp.exp(lg - m)
    s = jnp.sum(e, axis=-1, keepdims=True)
    return e / jnp.where(s > 0, s, 1.0)


def _gather_pages(pool, page_table):
    g = pool.reshape(pool.shape[0], -1)[page_table]
    return g.reshape((page_table.shape[0], page_table.shape[1] * PAGE_SIZE) + pool.shape[2:])


def _t5_bias_hm(table, dist):
    return jnp.moveaxis(table[_t5_bucket(dist)], -1, 0).astype(F32)


def _decode_even(dm, q_s, gf_s, kv_cmp_s, kv_sel_s, kv_win_s, kv_b_s, logf_s,
                 pool_cmp, pool_sel, pool_fox_kv, pool_fox_logf, win_buf, page_table, w_cmp, pe_cmp, t5_table):
    DB, DT = dm.DB, dm.DT
    past = page_table.shape[1] * PAGE_SIZE
    total = past + DT
    padded = -(-total // L_SEL) * L_SEL
    qpos = past + jnp.arange(DT, dtype=jnp.int32)
    q = q_s.astype(F32).reshape(H_A + H_B, DB, DT, HEAD_DIM)
    qa = q[:H_A].reshape(G_A, R_A, DB, DT, HEAD_DIM)
    qb = q[H_A:].reshape(G_B, R_B, DB, DT, HEAD_DIM)
    kvs = lambda a: a.reshape(DB, DT, 2, G_A, HEAD_DIM)
    rows = jnp.concatenate([_gather_pages(pool_cmp, page_table), kvs(kv_cmp_s)], axis=1)
    rows = jnp.pad(rows, ((0, 0), (0, padded - total), (0, 0), (0, 0), (0, 0)))
    blk = rows.reshape(DB, padded // L_CMP, L_CMP, 2, G_A, HEAD_DIM)
    blk = blk + jnp.transpose(pe_cmp, (1, 0, 2))[None, None, :, :, None, :]
    cmp_blk = jnp.einsum('bnlkgd,kldo->bnkgo', blk, w_cmp)
    ck, cv = cmp_blk[:, :, 0], cmp_blk[:, :, 1]
    nc = ck.shape[1]
    c_end = (jnp.arange(nc, dtype=jnp.int32) + 1) * L_CMP - 1
    lg = jnp.einsum('grbqd,bngd->grbqn', qa, ck, preferred_element_type=F32)
    bias = _t5_bias_hm(t5_table, qpos[:, None] - c_end[None, :]).reshape(G_A, R_A, 1, DT, nc)
    p_cmp = _masked_softmax(lg + bias, (c_end[None, :] <= qpos[:, None])[None, None, None])
    o_cmp = jnp.einsum('grbqn,bngd->grbqd', p_cmp, cv)
    ns = nc * L_CMP // L_SEL
    imp = jnp.sum(p_cmp, axis=1).reshape(G_A, DB, DT, ns, L_SEL // L_CMP).sum(-1)
    bl = jnp.arange(ns, dtype=jnp.int32)[None, :]
    cur = (qpos // L_SEL)[:, None]
    valid = bl <= cur
    forced = (bl == 0) | (bl == cur) | (bl == cur - 1)
    score = jnp.where(forced & valid, jnp.inf, jnp.where(valid, imp, -jnp.inf))
    _, idx = lax.top_k(score, min(N_SEL, ns))
    chosen = jnp.sum(jax.nn.one_hot(idx, ns, dtype=F32), axis=-2) > 0
    rows_sel = jnp.concatenate([_gather_pages(pool_sel, page_table), kvs(kv_sel_s)], axis=1)
    kpos = jnp.arange(total, dtype=jnp.int32)
    dist = qpos[:, None] - kpos[None, :]
    ok = jnp.repeat(chosen, L_SEL, axis=-1)[..., :total] & (dist >= 0)[None, None]
    lg = jnp.einsum('grbqd,bsgd->grbqs', qa, rows_sel[:, :, 0], preferred_element_type=F32)
    lg = lg + _t5_bias_hm(t5_table, dist).reshape(G_A, R_A, 1, DT, total)
    p = _masked_softmax(lg, ok[:, None])
    o_sel = jnp.einsum('grbqs,bsgd->grbqd', p, rows_sel[:, :, 1])
    wb = win_buf.shape[1]
    win_rows = jnp.concatenate([win_buf, kvs(kv_win_s)], axis=1)
    wpos = past - wb + jnp.arange(wb + DT, dtype=jnp.int32)
    wdist = qpos[:, None] - wpos[None, :]
    lg = jnp.einsum('grbqd,bwgd->grbqw', qa, win_rows[:, :, 0], preferred_element_type=F32)
    lg = lg + _t5_bias_hm(t5_table, wdist).reshape(G_A, R_A, 1, DT, wb + DT)
    p = _masked_softmax(lg, ((wdist >= 0) & (wdist < WINDOW) & (wpos[None, :] >= 0))[None, None, None])
    o_win = jnp.einsum('grbqw,bwgd->grbqd', p, win_rows[:, :, 1])
    logf_all = jnp.concatenate([_gather_pages(pool_fox_logf, page_table), logf_s.reshape(DB, DT, H_B)], axis=1)
    cum = jnp.cumsum(logf_all, axis=1)
    kv_all = jnp.concatenate([_gather_pages(pool_fox_kv, page_table), kv_b_s.reshape(DB, DT, 2, G_B, HEAD_DIM)],
                             axis=1)
    lg = jnp.einsum('grbqd,bsgd->grbqs', qb, kv_all[:, :, 0], preferred_element_type=F32)
    cq = jnp.transpose(cum[:, past:].reshape(DB, DT, G_B, R_B), (2, 3, 0, 1))[..., None]
    ckk = jnp.transpose(cum.reshape(DB, total, G_B, R_B), (2, 3, 0, 1))[..., None, :]
    p = _masked_softmax(lg + (cq - ckk), (dist >= 0)[None, None, None])
    o_fox = jnp.einsum('grbqs,bsgd->grbqd', p, kv_all[:, :, 1])
    hm = lambda o: o.reshape(H_A, DB * DT, HEAD_DIM)
    return hm(o_cmp), hm(o_sel), hm(o_win), hm(o_fox), win_rows[:, DT:]


def _decode_odd(dm, q_lat_s, q_pe_s, ckv_s, kpe_s, pool_ckv, pool_kpe, page_table):
    DB, DT = dm.DB, dm.DT
    past = page_table.shape[1] * PAGE_SIZE
    qpos = past + jnp.arange(DT, dtype=jnp.int32)
    q_lat = q_lat_s.astype(F32).reshape(DB, DT, H_C, KV_LORA)
    q_pe = q_pe_s.astype(F32).reshape(DB, DT, H_C, D_ROPE)
    ckv_all = jnp.concatenate([_gather_pages(pool_ckv, page_table), ckv_s.reshape(DB, DT, KV_LORA)], axis=1)
    kpe_all = jnp.concatenate([_gather_pages(pool_kpe, page_table), kpe_s.reshape(DB, DT, D_ROPE)], axis=1)
    kpos = jnp.arange(past + DT, dtype=jnp.int32)
    lg = (jnp.einsum('bqhc,bsc->bhqs', q_lat, ckv_all, preferred_element_type=F32)
          + jnp.einsum('bqhr,bsr->bhqs', q_pe, kpe_all, preferred_element_type=F32))
    p = _masked_softmax(lg, kpos[None, :] <= qpos[:, None])
    o = jnp.einsum('bhqs,bsc->bqhc', p, ckv_all)
    return _bf(o.reshape(DB * DT * H_C, KV_LORA))


def kernel(x_prompt, x_sample, c_prompt, c_sample, cache_nsa_cmp, cache_nsa_sel, cache_fox_kv, cache_fox_logf, state_nsa_win, cache_mla_ckv, cache_mla_kpe, page_table, t5_table, w_in_ab, b_forget, w_cmp, pe_cmp, w_out_ab, w_in_c, g_qnorm, w_qb, g_kvnorm, w_kvb, w_out_c, g_norm_mix, g_norm_ffn, w_ada, b_ada, w_router, b_router, w_moe_gu, b_moe_gu, w_moe_down, b_moe_down, g_final):
    B, T, _ = x_prompt.shape
    DB, DT, _ = x_sample.shape
    dm = Dims(B, T, DB, DT)
    Mp, M = dm.Mp, dm.M
    depth = w_ada.shape[0]
    past = page_table.shape[1] * PAGE_SIZE
    kv5 = (2, G_A, HEAD_DIM)

    x = jnp.concatenate([x_prompt.reshape(Mp, D_MODEL), x_sample.reshape(dm.Ms, D_MODEL)], axis=0)
    mod = ada_call(jnp.concatenate([c_prompt, c_sample], axis=0), w_ada, b_ada)

    def mod_vec(layer, k):
        m = mod[layer, :, k * D_MODEL:(k + 1) * D_MODEL]
        return m[:B].reshape(B, 1, D_MODEL), jnp.repeat(m[B:], DT, axis=0)

    strips = t5_strips_call(t5_table)
    cos32, sin32, cos_h, sin_h = _rope_tables(dm, past)

    even_p, even_s, odd_p, odd_s = [], [], [], []
    for layer in range(depth):
        shift0, scale0, gate0, shift1, scale1, gate1 = [mod_vec(layer, k) for k in range(6)]
        g_mix = g_norm_mix[layer][None]
        if layer % 2 == 0:
            e = layer // 2
            w_perm, bf_row, w_big, pe_flat, w_out_h = _prep_even(w_in_ab[e], b_forget[e], w_cmp[e], pe_cmp[e],
                                                                 w_out_ab[e])
            kv_cmp, kv_sel, kv_win, kv_b, gf, logf, q_hm, kvh = proj_even_call(dm, x, g_mix, shift0, scale0,
                                                                              w_perm, bf_row)
            _, cmp_hm = compress_call(kv_cmp[:Mp].reshape(Mp // L_CMP, L_CMP * KV_W), pe_flat, w_big)
            o_cmp, selneg = cmp_select_prompt_call(dm, t5_table, q_hm, cmp_hm)
            o_sel = flash_gqa_call(dm, "sel", q_hm, 0, kvh, 0, kvh, 2, H_A, selneg=selneg, strips=strips)
            o_win = flash_gqa_call(dm, "win", q_hm, 0, kvh, 4, kvh, 6, H_A, strips=strips)
            logf_t = jnp.transpose(logf[:Mp].reshape(B, T, H_B), (0, 2, 1)).reshape(B * H_B, T)
            cum = jnp.transpose(cumsum_call(logf_t).reshape(B, H_B, T), (0, 2, 1))
            q_fox, k_fox = _fox_augment(dm, q_hm, kvh, cum)
            o_fox = flash_gqa_call(dm, "fox", q_fox, 0, k_fox, 0, kvh, 10, H_B)
            d_cmp, d_sel, d_win, d_fox, win_new = _decode_even(
                dm, q_hm[:, Mp:], gf[Mp:], kv_cmp[Mp:], kv_sel[Mp:], kv_win[Mp:], kv_b[Mp:], logf[Mp:],
                cache_nsa_cmp[e], cache_nsa_sel[e], cache_fox_kv[e], cache_fox_logf[e], state_nsa_win[e],
                page_table, w_cmp[e], pe_cmp[e], t5_table)
            put = lambda o, d: lax.dynamic_update_slice(o, d.astype(o.dtype), (0, Mp, 0))
            o_cmp, o_sel, o_win, o_fox = put(o_cmp, d_cmp), put(o_sel, d_sel), put(o_win, d_win), put(o_fox, d_fox)
            x = out_even_call(dm, x, gate0, gf, o_cmp, o_sel, o_win, o_fox, w_out_h)
            n_keep = min(WINDOW, T)
            sp = lambda a: a[:Mp].reshape((B, T) + kv5)
            ss = lambda a: a[Mp:].reshape((DB, DT) + kv5)
            even_p.append((sp(kv_cmp), sp(kv_sel), sp(kv_b), logf[:Mp].reshape(B, T, H_B),
                           sp(kv_win)[:, T - n_keep:]))
            even_s.append((ss(kv_cmp), ss(kv_sel), ss(kv_b), logf[Mp:].reshape(DB, DT, H_B), win_new))
        else:
            o = layer // 2
            w_perm, w_qabs, w_qpe, w_comb = _prep_odd(w_in_c[o], w_qb[o], w_kvb[o], w_out_c[o])
            cqn, ckv, ckv_b, kpe, kpe_b = proj_odd_call(dm, x, g_mix, shift0, scale0, w_perm,
                                                        g_qnorm[o][None], g_kvnorm[o][None], cos32, sin32)
            q_lat, q_pe = q_mla_call(dm, cqn, w_qabs, w_qpe, cos_h, sin_h)
            q_lat = q_lat.reshape(M * H_C, KV_LORA)
            q_pe = q_pe.reshape(M * H_C, D_ROPE)
            o_lat = flash_mla_call(dm, q_lat, q_pe, ckv_b, kpe_b)
            d_lat = _decode_odd(dm, q_lat[Mp * H_C:], q_pe[Mp * H_C:], ckv[Mp:], kpe[Mp:],
                                cache_mla_ckv[o], cache_mla_kpe[o], page_table)
            o_lat = lax.dynamic_update_slice(o_lat, d_lat, (Mp * H_C, 0))
            x = out_odd_call(dm, x, gate0, o_lat.reshape(M, H_C * KV_LORA), w_comb)
            odd_p.append((ckv[:Mp].reshape(B, T, KV_LORA), kpe[:Mp].reshape(B, T, D_ROPE)))
            odd_s.append((ckv[Mp:].reshape(DB, DT, KV_LORA), kpe[Mp:].reshape(DB, DT, D_ROPE)))
        h, gates = moe_pre_call(dm, x, g_norm_ffn[layer][None], shift1, scale1, w_router[layer],
                                b_router[layer][None])
        x = moe_dense_call(dm, h, gates, _bf(w_moe_gu[layer]), b_moe_gu[layer], _bf(w_moe_down[layer]),
                           b_moe_down[layer], x, gate1)

    y = final_norm_call(dm, x, g_final[None])
    outs = [y[:Mp].reshape(B, T, D_MODEL), y[Mp:].reshape(DB, DT, D_MODEL)]
    ep = [jnp.stack(a) for a in zip(*even_p)]
    es = [jnp.stack(a) for a in zip(*even_s)]
    op = [jnp.stack(a) for a in zip(*odd_p)]
    os_ = [jnp.stack(a) for a in zip(*odd_s)]
    for a, b in zip(ep, es):
        outs += [a, b]
    for a, b in zip(op, os_):
        outs += [a, b]
    return tuple(outs)
```

```python
import functools
import math

import numpy as np
import jax
import jax.numpy as jnp
from jax import lax
from jax.experimental import pallas as pl
from jax.experimental.pallas import tpu as pltpu

F32 = jnp.float32
BF16 = jnp.bfloat16

D_MODEL = 1024
HEAD_DIM = 64
PAGE_SIZE = 128
H_A, G_A = 8, 2
R_A = H_A // G_A
L_CMP, L_SEL, N_SEL, WINDOW = 32, 64, 16, 512
H_B, G_B = 8, 2
R_B = H_B // G_B
H_C, Q_LORA, KV_LORA, D_NOPE, D_ROPE, D_V = 16, 512, 256, 64, 32, 64
ROPE_THETA = 10000.0
MLA_SCALE = (D_NOPE + D_ROPE) ** -0.5
N_BUCKETS, MAX_DISTANCE = 32, 128
TOP_K = 4
SWIGLU_LIMIT, SWIGLU_ALPHA = 7.0, 1.702
RMS_EPS = 1e-6
KV_W = 2 * G_A * HEAD_DIM

NEG_BIG = -1e30
SEL_NEG = -30000.0
TQ = 128
TK = 512
VMEM_LIMIT = 56 * 1024 * 1024


def _bf(x):
    return x.astype(BF16)


def _dot(a, b):
    return jnp.dot(_bf(a), _bf(b), preferred_element_type=F32)


def _dot_nt(a, b):
    return lax.dot_general(_bf(a), _bf(b), (((1,), (1,)), ((), ())), preferred_element_type=F32)


def _split2(x):
    hi = _bf(x)
    return hi, _bf(x - hi.astype(F32))


def _split3(x):
    hi = _bf(x)
    r = x - hi.astype(F32)
    mid = _bf(r)
    return hi, mid, _bf(r - mid.astype(F32))


def _dot3(a, b):
    ah, al = _split2(a)
    bh, bl = _split2(b)
    d = functools.partial(jnp.dot, preferred_element_type=F32)
    return d(ah, bh) + d(ah, bl) + d(al, bh)


def _dot3_nt(a, b):
    ah, al = _split2(a)
    bh, bl = _split2(b)
    d = functools.partial(lax.dot_general, dimension_numbers=(((1,), (1,)), ((), ())),
                          preferred_element_type=F32)
    return d(ah, bh) + d(ah, bl) + d(al, bh)


def _rms(x, g):
    return x * lax.rsqrt(jnp.mean(x * x, axis=-1, keepdims=True) + RMS_EPS) * g


def _log_sigmoid(z):
    return jnp.minimum(z, 0.0) - jnp.log1p(jnp.exp(-jnp.abs(z)))


def _t5_bucket(dist):
    n = jnp.maximum(dist, 0)
    max_exact = N_BUCKETS // 2
    nf = jnp.maximum(n, 1).astype(F32)
    large = max_exact + (jnp.log(nf / max_exact) / math.log(MAX_DISTANCE / max_exact)
                         * (N_BUCKETS - max_exact)).astype(jnp.int32)
    return jnp.where(n < max_exact, n, jnp.minimum(large, N_BUCKETS - 1))


def _params(sem, vmem=VMEM_LIMIT):
    return pltpu.CompilerParams(dimension_semantics=sem, vmem_limit_bytes=vmem)


class Dims:
    def __init__(self, B, T, DB, DT):
        self.B, self.T, self.DB, self.DT = B, T, DB, DT
        self.Mp, self.Ms = B * T, DB * DT
        self.M = self.Mp + self.Ms
        self.tm = math.gcd(512, self.Ms)
        assert T % self.tm == 0 and T % TK == 0 and self.tm % 8 == 0
        self.n_p = self.Mp // self.tm
        self.n_t = self.M // self.tm
        self.tiles_per_b = T // self.tm

    def mod_specs(self):
        tpb, nb, n_p, tm = self.tiles_per_b, self.B, self.n_p, self.tm
        sp = pl.BlockSpec((1, 1, D_MODEL), lambda i, *_: (jnp.minimum(i // tpb, nb - 1), 0, 0))
        ss = pl.BlockSpec((tm, D_MODEL), lambda i, *_: (jnp.maximum(i - n_p, 0), 0))
        return sp, ss


def _pick_mod(i, n_p, mp_ref, ms_ref):
    return jnp.where(i < n_p, mp_ref[0], ms_ref[...])


def _ada_kernel(c_ref, w_ref, b_ref, o_ref):
    c = c_ref[...]
    o_ref[0] = _dot3(c * jax.nn.sigmoid(c), w_ref[0]) + b_ref[0]


def ada_call(c_all, w_ada, b_ada):
    depth, _, n6 = w_ada.shape
    n_c = c_all.shape[0]
    tn = 1536
    return pl.pallas_call(
        _ada_kernel,
        grid=(depth, n6 // tn),
        in_specs=[pl.BlockSpec((n_c, D_MODEL), lambda l, j: (0, 0)),
                  pl.BlockSpec((1, D_MODEL, tn), lambda l, j: (l, 0, j)),
                  pl.BlockSpec((1, 1, tn), lambda l, j: (l, 0, j))],
        out_specs=pl.BlockSpec((1, n_c, tn), lambda l, j: (l, 0, j)),
        out_shape=jax.ShapeDtypeStruct((depth, n_c, n6), F32),
        compiler_params=_params(("arbitrary", "arbitrary")),
        name="ada_modulation",
    )(c_all, w_ada, b_ada.reshape(depth, 1, n6))


def _t5_strip_kernel(tab_ref, o_ref):
    p = pl.program_id(0)
    i = lax.broadcasted_iota(jnp.int32, (TQ, TK), 0)
    j = lax.broadcasted_iota(jnp.int32, (TQ, TK), 1)
    off = jnp.where(p < 4, p * TQ, jnp.where(p == 4, TK, 4 * TK))
    bucket = _t5_bucket(off + i - j)
    for h in range(H_A):
        acc = jnp.zeros((TQ, TK), F32)
        for b in range(N_BUCKETS):
            acc = jnp.where(bucket == b, tab_ref[b, h], acc)
        o_ref[h, 0] = acc


def t5_strips_call(t5_table):
    return pl.pallas_call(
        _t5_strip_kernel,
        grid=(6,),
        in_specs=[pl.BlockSpec(memory_space=pltpu.SMEM)],
        out_specs=pl.BlockSpec((H_A, 1, TQ, TK), lambda p: (0, p, 0, 0)),
        out_shape=jax.ShapeDtypeStruct((H_A, 6, TQ, TK), F32),
        compiler_params=_params(("arbitrary",)),
        name="t5_strips",
    )(t5_table)


def _proj_even_kernel(x_ref, g_ref, shp_ref, shs_ref, scp_ref, scs_ref, w_ref, bf_ref,
                      cmp_ref, sel_ref, win_ref, kvb_ref, gf_ref, logf_ref, q_ref, kvh_ref, *, n_p):
    i = pl.program_id(0)
    shift = _pick_mod(i, n_p, shp_ref, shs_ref)
    scale = _pick_mod(i, n_p, scp_ref, scs_ref)
    hb = _bf(_rms(x_ref[...], g_ref[...]) * (1.0 + scale) + shift)
    q = jnp.dot(hb, w_ref[:, 0:1024], preferred_element_type=F32) * (HEAD_DIM ** -0.5)
    for h in range(H_A + H_B):
        q_ref[h] = _bf(q[:, h * HEAD_DIM:(h + 1) * HEAD_DIM])
    kv = jnp.dot(hb, w_ref[:, 1024:2048], preferred_element_type=F32)
    cmp_ref[...] = kv[:, 0:256]
    sel_ref[...] = kv[:, 256:512]
    win_ref[...] = kv[:, 512:768]
    kvb_ref[...] = kv[:, 768:1024]
    for c in range(12):
        kvh_ref[c] = _bf(kv[:, 256 + c * HEAD_DIM:256 + (c + 1) * HEAD_DIM])
    gfr = jnp.dot(hb, w_ref[:, 2048:2176], preferred_element_type=F32)
    lane = lax.broadcasted_iota(jnp.int32, gfr.shape, 1)
    lsg = _log_sigmoid(gfr + bf_ref[...])
    gf_ref[...] = jnp.where(lane < H_B, lsg, jax.nn.sigmoid(gfr))
    logf_ref[...] = lsg[:, 0:H_B]


def proj_even_call(dm, x, g, shift, scale, w_perm, bf_row):
    tm, M = dm.tm, dm.M
    sp, ss = dm.mod_specs()
    row = lambda w: pl.BlockSpec((tm, w), lambda i: (i, 0))
    outs = [jax.ShapeDtypeStruct((M, KV_W), F32)] * 4 + [
        jax.ShapeDtypeStruct((M, 128), F32), jax.ShapeDtypeStruct((M, H_B), F32),
        jax.ShapeDtypeStruct((H_A + H_B, M, HEAD_DIM), BF16),
        jax.ShapeDtypeStruct((12, M, HEAD_DIM), BF16)]
    return pl.pallas_call(
        functools.partial(_proj_even_kernel, n_p=dm.n_p),
        grid=(dm.n_t,),
        in_specs=[row(D_MODEL), pl.BlockSpec((1, D_MODEL), lambda i: (0, 0)), sp, ss, sp, ss,
                  pl.BlockSpec((D_MODEL, 2176), lambda i: (0, 0)),
                  pl.BlockSpec((1, 128), lambda i: (0, 0))],
        out_specs=[row(KV_W)] * 4 + [row(128), row(H_B),
                   pl.BlockSpec((H_A + H_B, tm, HEAD_DIM), lambda i: (0, i, 0)),
                   pl.BlockSpec((12, tm, HEAD_DIM), lambda i: (0, i, 0))],
        out_shape=outs,
        compiler_params=_params(("arbitrary",)),
        name="proj_even",
    )(x, g, shift[0], shift[1], scale[0], scale[1], w_perm, bf_row)


def _compress_kernel(x_ref, pe_ref, w_ref, o_ref, oh_ref):
    y = _dot(x_ref[...] + pe_ref[...], w_ref[...])
    o_ref[...] = y
    for c in range(4):
        oh_ref[c] = _bf(y[:, c * HEAD_DIM:(c + 1) * HEAD_DIM])


def compress_call(rows_blk, pe_flat, w_big):
    n_blk, kdim = rows_blk.shape
    tb = math.gcd(n_blk, 256)
    return pl.pallas_call(
        _compress_kernel,
        grid=(n_blk // tb,),
        in_specs=[pl.BlockSpec((tb, kdim), lambda i: (i, 0)),
                  pl.BlockSpec((1, kdim), lambda i: (0, 0)),
                  pl.BlockSpec((kdim, KV_W), lambda i: (0, 0))],
        out_specs=[pl.BlockSpec((tb, KV_W), lambda i: (i, 0)),
                   pl.BlockSpec((4, tb, HEAD_DIM), lambda i: (0, i, 0))],
        out_shape=[jax.ShapeDtypeStruct((n_blk, KV_W), F32),
                   jax.ShapeDtypeStruct((4, n_blk, HEAD_DIM), BF16)],
        compiler_params=_params(("arbitrary",)),
        name="compress",
    )(rows_blk, pe_flat, w_big)


def _cumsum_kernel(x_ref, o_ref, carry_ref):
    j = pl.program_id(0)

    @pl.when(j == 0)
    def _():
        carry_ref[...] = jnp.zeros_like(carry_ref)

    n = x_ref.shape[1]
    u = lax.broadcasted_iota(jnp.int32, (n, n), 0)
    s = lax.broadcasted_iota(jnp.int32, (n, n), 1)
    tri = (u <= s).astype(BF16)
    hi, mid, lo = _split3(x_ref[...])
    d = functools.partial(jnp.dot, preferred_element_type=F32)
    c = d(hi, tri) + d(mid, tri) + d(lo, tri) + carry_ref[...]
    o_ref[...] = c
    carry_ref[...] = c[:, n - 1:n]


def cumsum_call(x):
    rows, n = x.shape
    tn = math.gcd(n, 512)
    return pl.pallas_call(
        _cumsum_kernel,
        grid=(n // tn,),
        in_specs=[pl.BlockSpec((rows, tn), lambda j: (0, j))],
        out_specs=pl.BlockSpec((rows, tn), lambda j: (0, j)),
        out_shape=jax.ShapeDtypeStruct((rows, n), F32),
        scratch_shapes=[pltpu.VMEM((rows, 1), F32)],
        compiler_params=_params(("arbitrary",)),
        name="cumsum_logf",
    )(x)


def _cmp_select_kernel(tab_ref, q_ref, ck_ref, cv_ref, o_ref, sel_ref, *, n_sel_blocks, k_sel, pos0, pos_mod):
    g = pl.program_id(0) % G_A
    qi = pl.program_id(1)
    tq = q_ref.shape[1]
    nc = ck_ref.shape[1]
    row = lax.broadcasted_iota(jnp.int32, (tq, nc), 0)
    if pos_mod is None:
        tpos = pos0 + qi * tq + row
    else:
        tpos = pos0 + row % pos_mod
    c_end = (lax.broadcasted_iota(jnp.int32, (tq, nc), 1) + 1) * L_CMP - 1
    vis = c_end <= tpos
    bucket = _t5_bucket(tpos - c_end)
    q = q_ref[...]
    lg = _dot_nt(q.reshape(R_A * tq, HEAD_DIM), ck_ref[0]).reshape(R_A, tq, nc)
    imp = jnp.zeros((tq, nc), F32)
    for r in range(R_A):
        bias = jnp.zeros((tq, nc), F32)
        for b in range(N_BUCKETS):
            bias = jnp.where(bucket == b, tab_ref[b, g * R_A + r], bias)
        lr = jnp.where(vis, lg[r] + bias, NEG_BIG)
        m = jnp.max(lr, axis=-1, keepdims=True)
        e = jnp.where(vis, jnp.exp(lr - m), 0.0)
        s = jnp.sum(e, axis=-1, keepdims=True)
        p = e / jnp.where(s > 0, s, 1.0)
        o_ref[r] = _dot(p, cv_ref[0])
        imp = imp + p
    nsp = sel_ref.shape[2]
    cpb = L_SEL // L_CMP
    pr = lax.broadcasted_iota(jnp.int32, (nc, nsp), 0) // cpb
    pc = lax.broadcasted_iota(jnp.int32, (nc, nsp), 1)
    pair = (pr == pc).astype(BF16)
    hi, mid, lo = _split3(imp)
    d = functools.partial(jnp.dot, preferred_element_type=F32)
    imp_s = d(hi, pair) + d(mid, pair) + d(lo, pair)
    blk = lax.broadcasted_iota(jnp.int32, (tq, nsp), 1)
    cur = tpos[:, 0:1] // L_SEL
    valid = (blk <= cur) & (blk < n_sel_blocks)
    forced = (blk == 0) | (blk == cur) | (blk == cur - 1)
    score = jnp.where(forced & valid, jnp.inf, jnp.where(valid, imp_s, -jnp.inf))
    chosen = jnp.zeros((tq, nsp), jnp.bool_)
    for _ in range(k_sel):
        m = jnp.max(score, axis=-1, keepdims=True)
        idx = jnp.min(jnp.where(score == m, blk, nsp), axis=-1, keepdims=True)
        hit = blk == idx
        chosen = chosen | hit
        score = jnp.where(hit, -jnp.inf, score)
    sel_ref[0] = jnp.where(chosen, 0.0, SEL_NEG).astype(BF16)


def cmp_select_prompt_call(dm, t5_table, q_hm, cmp_hm):
    B, T, M = dm.B, dm.T, dm.M
    nq, nc = T // TQ, T // L_CMP
    ns = T // L_SEL
    nsp = max(128, ns)
    kern = functools.partial(_cmp_select_kernel, n_sel_blocks=ns, k_sel=min(N_SEL, ns), pos0=0, pos_mod=None)
    return pl.pallas_call(
        kern,
        grid=(B * G_A, nq),
        in_specs=[pl.BlockSpec(memory_space=pltpu.SMEM),
                  pl.BlockSpec((R_A, TQ, HEAD_DIM), lambda bg, qi: (bg % G_A, (bg // G_A) * nq + qi, 0)),
                  pl.BlockSpec((1, nc, HEAD_DIM), lambda bg, qi: (bg % G_A, bg // G_A, 0)),
                  pl.BlockSpec((1, nc, HEAD_DIM), lambda bg, qi: (G_A + bg % G_A, bg // G_A, 0))],
        out_specs=[pl.BlockSpec((R_A, TQ, HEAD_DIM), lambda bg, qi: (bg % G_A, (bg // G_A) * nq + qi, 0)),
                   pl.BlockSpec((1, TQ, nsp), lambda bg, qi: (bg % G_A, (bg // G_A) * nq + qi, 0))],
        out_shape=[jax.ShapeDtypeStruct((H_A, dm.Mp, HEAD_DIM), F32),
                   jax.ShapeDtypeStruct((G_A, dm.Mp, nsp), BF16)],
        compiler_params=_params(("arbitrary", "arbitrary")),
        name="nsa_cmp_select_prompt",
    )(t5_table, q_hm, cmp_hm, cmp_hm)


def _steps(T, window):
    qt, kt, ft = [], [], []
    for qi in range(T // TQ):
        kd = (qi * TQ) // TK
        p = (qi * TQ % TK) // TQ
        k0 = max(kd - 1, 0) if window else 0
        for ki in range(k0, kd + 1):
            strip = p if ki == kd else (4 if (ki == kd - 1 and p == 0) else 5)
            fl = (1 if ki == k0 else 0) | (2 if ki == kd else 0) | (4 if ki == kd else 0) | (strip << 4)
            qt.append(qi), kt.append(ki), ft.append(fl)
    return (np.asarray(qt, np.int32), np.asarray(kt, np.int32), np.asarray(ft, np.int32))


def _flash_kernel(qt_ref, kt_ref, ft_ref, *refs, mode, rows_per_tok):
    if mode == "mla":
        q_ref, qpe_ref, k_ref, kpe_ref, o_ref, m_s, l_s, acc_s = refs
    elif mode == "sel":
        q_ref, k_ref, v_ref, selneg_ref, strip_ref, o_ref, m_s, l_s, acc_s = refs
    elif mode == "win":
        q_ref, k_ref, v_ref, strip_ref, o_ref, m_s, l_s, acc_s = refs
    else:
        q_ref, k_ref, v_ref, o_ref, m_s, l_s, acc_s = refs
    s = pl.program_id(1)
    qi, ki, fl = qt_ref[s], kt_ref[s], ft_ref[s]
    rows = m_s.shape[0]
    tk = k_ref.shape[-2]

    @pl.when((fl & 1) != 0)
    def _init():
        m_s[...] = jnp.full_like(m_s, NEG_BIG)
        l_s[...] = jnp.zeros_like(l_s)
        acc_s[...] = jnp.zeros_like(acc_s)

    def step(masked):
        if mode == "mla":
            kb = k_ref[...]
            lg = _dot_nt(q_ref[...], kb) + _dot_nt(qpe_ref[...], kpe_ref[...])
            vb = kb
        else:
            q = q_ref[...]
            lg = _dot_nt(q.reshape(rows, q.shape[-1]), k_ref[0])
            vb = v_ref[0]
        if mode == "sel":
            nsp = selneg_ref.shape[2]
            blk = lax.broadcasted_iota(jnp.int32, (nsp, tk), 0)
            key = ki * tk + lax.broadcasted_iota(jnp.int32, (nsp, tk), 1)
            onehot = (blk == key // L_SEL).astype(BF16)
            sb = jnp.dot(selneg_ref[0], onehot, preferred_element_type=F32)
            lg = (lg.reshape(R_A, TQ, tk) + sb[None]).reshape(rows, tk)
        if mode in ("sel", "win"):
            lg = lg + strip_ref[:, 0].reshape(rows, tk)
        if masked:
            r = lax.broadcasted_iota(jnp.int32, (rows, tk), 0)
            tok = r // rows_per_tok if rows_per_tok > 1 else r % TQ
            dist = (qi * TQ + tok) - (ki * tk + lax.broadcasted_iota(jnp.int32, (rows, tk), 1))
            ok = dist >= 0
            if mode == "win":
                ok = ok & (dist < WINDOW)
            lg = jnp.where(ok, lg, NEG_BIG)
        m_prev = m_s[...]
        m_new = jnp.maximum(m_prev, jnp.max(lg, axis=-1, keepdims=True))
        alpha = jnp.exp(m_prev - m_new)
        p = jnp.exp(lg - m_new)
        l_s[...] = alpha * l_s[...] + jnp.sum(p, axis=-1, keepdims=True)
        acc_s[...] = alpha * acc_s[...] + _dot(p, vb)
        m_s[...] = m_new

    if mode == "win":
        step(True)
    else:
        @pl.when((fl & 4) != 0)
        def _diag():
            step(True)

        @pl.when((fl & 4) == 0)
        def _inner():
            step(False)

    @pl.when((fl & 2) != 0)
    def _fin():
        o = acc_s[...] / l_s[...]
        o_ref[...] = o.reshape(o_ref.shape).astype(o_ref.dtype)


def _flash_scratch(rows, dv):
    return [pltpu.VMEM((rows, 1), F32), pltpu.VMEM((rows, 1), F32), pltpu.VMEM((rows, dv), F32)]


def flash_gqa_call(dm, mode, q, q_head0, k, k_idx0, v, v_idx0, n_heads_out, selneg=None, strips=None):
    B, T, M = dm.B, dm.T, dm.M
    R, G = R_A, G_A
    nq, nk = T // TQ, T // TK
    qt, kt, ft = _steps(T, window=(mode == "win"))
    dq, dk, dv = q.shape[-1], k.shape[-1], v.shape[-1]
    qh0 = q_head0 // R
    in_specs = [
        pl.BlockSpec((R, TQ, dq), lambda bg, s, qt, kt, ft: (qh0 + bg % G, (bg // G) * nq + qt[s], 0)),
        pl.BlockSpec((1, TK, dk), lambda bg, s, qt, kt, ft: (k_idx0 + bg % G, (bg // G) * nk + kt[s], 0)),
        pl.BlockSpec((1, TK, dv), lambda bg, s, qt, kt, ft: (v_idx0 + bg % G, (bg // G) * nk + kt[s], 0)),
    ]
    args = [q, k, v]
    if mode == "sel":
        nsp = selneg.shape[-1]
        in_specs.append(pl.BlockSpec((1, TQ, nsp), lambda bg, s, qt, kt, ft: (bg % G, (bg // G) * nq + qt[s], 0)))
        args.append(selneg)
    if mode in ("sel", "win"):
        in_specs.append(pl.BlockSpec((R, 1, TQ, TK), lambda bg, s, qt, kt, ft: (bg % G, ft[s] >> 4, 0, 0)))
        args.append(strips)
    grid_spec = pltpu.PrefetchScalarGridSpec(
        num_scalar_prefetch=3,
        grid=(B * G, len(qt)),
        in_specs=in_specs,
        out_specs=pl.BlockSpec((R, TQ, dv), lambda bg, s, qt, kt, ft: (bg % G, (bg // G) * nq + qt[s], 0)),
        scratch_shapes=_flash_scratch(R * TQ, dv),
    )
    return pl.pallas_call(
        functools.partial(_flash_kernel, mode=mode, rows_per_tok=1),
        grid_spec=grid_spec,
        out_shape=jax.ShapeDtypeStruct((n_heads_out, dm.Mp, dv), F32),
        compiler_params=_params(("arbitrary", "arbitrary")),
        name="flash_" + mode,
    )(jnp.asarray(qt), jnp.asarray(kt), jnp.asarray(ft), *args)


def flash_mla_call(dm, q_lat, q_pe, ckv, kpe):
    B, T, M = dm.B, dm.T, dm.M
    nq, nk = T // TQ, T // TK
    qt, kt, ft = _steps(T, window=False)
    rows = TQ * H_C
    grid_spec = pltpu.PrefetchScalarGridSpec(
        num_scalar_prefetch=3,
        grid=(B, len(qt)),
        in_specs=[
            pl.BlockSpec((rows, KV_LORA), lambda b, s, qt, kt, ft: (b * nq + qt[s], 0)),
            pl.BlockSpec((rows, D_ROPE), lambda b, s, qt, kt, ft: (b * nq + qt[s], 0)),
            pl.BlockSpec((TK, KV_LORA), lambda b, s, qt, kt, ft: (b * nk + kt[s], 0)),
            pl.BlockSpec((TK, D_ROPE), lambda b, s, qt, kt, ft: (b * nk + kt[s], 0)),
        ],
        out_specs=pl.BlockSpec((rows, KV_LORA), lambda b, s, qt, kt, ft: (b * nq + qt[s], 0)),
        scratch_shapes=_flash_scratch(rows, KV_LORA),
    )
    return pl.pallas_call(
        functools.partial(_flash_kernel, mode="mla", rows_per_tok=H_C),
        grid_spec=grid_spec,
        out_shape=jax.ShapeDtypeStruct((dm.Mp * H_C, KV_LORA), BF16),
        compiler_params=_params(("arbitrary", "arbitrary")),
        name="flash_mla",
    )(jnp.asarray(qt), jnp.asarray(kt), jnp.asarray(ft), q_lat, q_pe, ckv, kpe)


def _out_even_kernel(x_ref, gp_ref, gs_ref, gf_ref, *refs, n_p):
    w_ref, o_ref = refs[8], refs[9]
    i = pl.program_id(0)
    oc, os_, ow, ob = [lambda h, p=refs[k], s=refs[4 + k]: jnp.where(i < n_p, p[h], s[h]) for k in range(4)]
    gf = gf_ref[...]
    acc = jnp.zeros(x_ref.shape, F32)
    for h in range(H_A):
        a = (gf[:, H_B + h:H_B + h + 1] * oc(h)
             + gf[:, H_B + H_A + h:H_B + H_A + h + 1] * os_(h)
             + gf[:, H_B + 2 * H_A + h:H_B + 2 * H_A + h + 1] * ow(h))
        acc = acc + _dot(a, w_ref[h])
    for h in range(H_B):
        acc = acc + _dot(ob(h), w_ref[H_A + h])
    o_ref[...] = x_ref[...] + _pick_mod(i, n_p, gp_ref, gs_ref) * acc


def out_even_call(dm, x, gate, gf, o_prompt, o_sample, w_out_h):
    tm, M, n_p = dm.tm, dm.M, dm.n_p
    sp, ss = dm.mod_specs()
    hm_p = pl.BlockSpec((H_A, tm, HEAD_DIM), lambda i: (0, jnp.minimum(i, n_p - 1), 0))
    hm_s = pl.BlockSpec((H_A, tm, HEAD_DIM), lambda i: (0, jnp.maximum(i - n_p, 0), 0))
    return pl.pallas_call(
        functools.partial(_out_even_kernel, n_p=n_p),
        grid=(dm.n_t,),
        in_specs=[pl.BlockSpec((tm, D_MODEL), lambda i: (i, 0)), sp, ss,
                  pl.BlockSpec((tm, 128), lambda i: (i, 0))] + [hm_p] * 4 + [hm_s] * 4 + [
                  pl.BlockSpec((H_A + H_B, HEAD_DIM, D_MODEL), lambda i: (0, 0, 0))],
        out_specs=pl.BlockSpec((tm, D_MODEL), lambda i: (i, 0)),
        out_shape=jax.ShapeDtypeStruct((M, D_MODEL), F32),
        compiler_params=_params(("arbitrary",)),
        name="out_even",
    )(x, gate[0], gate[1], gf, *o_prompt, *o_sample, w_out_h)


def _proj_odd_kernel(x_ref, g_ref, shp_ref, shs_ref, scp_ref, scs_ref, w_ref, gq_ref, gkv_ref, cos_ref, sin_ref,
                     cqn_ref, ckv_ref, ckvb_ref, kpe_ref, kpeb_ref, *, n_p):
    i = pl.program_id(0)
    shift = _pick_mod(i, n_p, shp_ref, shs_ref)
    scale = _pick_mod(i, n_p, scp_ref, scs_ref)
    hb = _bf(_rms(x_ref[...], g_ref[...]) * (1.0 + scale) + shift)
    cq = jnp.dot(hb, w_ref[:, 0:Q_LORA], preferred_element_type=F32)
    cqn_ref[...] = _bf(_rms(cq, gq_ref[...]))
    ckv = _rms(jnp.dot(hb, w_ref[:, Q_LORA:Q_LORA + KV_LORA], preferred_element_type=F32), gkv_ref[...])
    ckv_ref[...] = ckv
    ckvb_ref[...] = _bf(ckv)
    pe = jnp.dot(hb, w_ref[:, Q_LORA + KV_LORA:Q_LORA + KV_LORA + 128], preferred_element_type=F32)
    kpe = pe[:, 0:D_ROPE] * cos_ref[...] + pe[:, D_ROPE:2 * D_ROPE] * sin_ref[...]
    kpe_ref[...] = kpe
    kpeb_ref[...] = _bf(kpe)


def proj_odd_call(dm, x, g, shift, scale, w_perm, g_q, g_kv, cos32, sin32):
    tm, M = dm.tm, dm.M
    sp, ss = dm.mod_specs()
    row = lambda w: pl.BlockSpec((tm, w), lambda i: (i, 0))
    const = lambda r, c: pl.BlockSpec((r, c), lambda i: (0, 0))
    return pl.pallas_call(
        functools.partial(_proj_odd_kernel, n_p=dm.n_p),
        grid=(dm.n_t,),
        in_specs=[row(D_MODEL), const(1, D_MODEL), sp, ss, sp, ss, const(D_MODEL, 896),
                  const(1, Q_LORA), const(1, KV_LORA), row(D_ROPE), row(D_ROPE)],
        out_specs=[row(Q_LORA), row(KV_LORA), row(KV_LORA), row(D_ROPE), row(D_ROPE)],
        out_shape=[jax.ShapeDtypeStruct((M, Q_LORA), BF16), jax.ShapeDtypeStruct((M, KV_LORA), F32),
                   jax.ShapeDtypeStruct((M, KV_LORA), BF16), jax.ShapeDtypeStruct((M, D_ROPE), F32),
                   jax.ShapeDtypeStruct((M, D_ROPE), BF16)],
        compiler_params=_params(("arbitrary",)),
        name="proj_odd",
    )(x, g, shift[0], shift[1], scale[0], scale[1], w_perm, g_q, g_kv, cos32, sin32)


def _bmm_nt_kernel(a_ref, b_ref, o_ref):
    o_ref[0] = _dot3_nt(a_ref[0], b_ref[0])


def _bmm_kernel(a_ref, b_ref, o_ref):
    o_ref[0] = _dot3(a_ref[0], b_ref[0])


def bmm_call(a, b, nt, name):
    H, m, _ = a.shape
    n = b.shape[1] if nt else b.shape[2]
    return pl.pallas_call(
        _bmm_nt_kernel if nt else _bmm_kernel,
        grid=(H,),
        in_specs=[pl.BlockSpec((1,) + a.shape[1:], lambda h: (h, 0, 0)),
                  pl.BlockSpec((1,) + b.shape[1:], lambda h: (h, 0, 0))],
        out_specs=pl.BlockSpec((1, m, n), lambda h: (h, 0, 0)),
        out_shape=jax.ShapeDtypeStruct((H, m, n), F32),
        compiler_params=_params(("arbitrary",)),
        name=name,
    )(a, b)


def _qlat_kernel(a_ref, w_ref, o_ref):
    o_ref[...] = _bf(jnp.dot(a_ref[...], w_ref[...], preferred_element_type=F32) * MLA_SCALE)


def _qpe_kernel(a_ref, w_ref, cos_ref, sin_ref, o_ref):
    y = jnp.dot(a_ref[...], w_ref[...], preferred_element_type=F32)
    n = o_ref.shape[1]
    o_ref[...] = _bf((y[:, 0:n] * cos_ref[...] + y[:, n:2 * n] * sin_ref[...]) * MLA_SCALE)


def q_mla_call(dm, cqn, w_qabs, w_qpe, cos_h, sin_h):
    tm, M = dm.tm, dm.M
    n_lat = H_C * KV_LORA
    n_pe = H_C * D_ROPE
    tn = 1024
    q_lat = pl.pallas_call(
        _qlat_kernel,
        grid=(dm.n_t, n_lat // tn),
        in_specs=[pl.BlockSpec((tm, Q_LORA), lambda i, j: (i, 0)),
                  pl.BlockSpec((Q_LORA, tn), lambda i, j: (0, j))],
        out_specs=pl.BlockSpec((tm, tn), lambda i, j: (i, j)),
        out_shape=jax.ShapeDtypeStruct((M, n_lat), BF16),
        compiler_params=_params(("arbitrary", "arbitrary")),
        name="q_lat",
    )(cqn, w_qabs)
    q_pe = pl.pallas_call(
        _qpe_kernel,
        grid=(dm.n_t,),
        in_specs=[pl.BlockSpec((tm, Q_LORA), lambda i: (i, 0)),
                  pl.BlockSpec((Q_LORA, 2 * n_pe), lambda i: (0, 0)),
                  pl.BlockSpec((tm, n_pe), lambda i: (i, 0)),
                  pl.BlockSpec((tm, n_pe), lambda i: (i, 0))],
        out_specs=pl.BlockSpec((tm, n_pe), lambda i: (i, 0)),
        out_shape=jax.ShapeDtypeStruct((M, n_pe), BF16),
        compiler_params=_params(("arbitrary",)),
        name="q_pe",
    )(cqn, w_qpe, cos_h, sin_h)
    return q_lat, q_pe


def _out_odd_kernel(x_ref, gp_ref, gs_ref, op_ref, os_ref, w_ref, o_ref, *, n_p):
    i = pl.program_id(0)
    o_lat = jnp.where(i < n_p, op_ref[...], os_ref[...])
    y = jnp.dot(o_lat, w_ref[...], preferred_element_type=F32)
    o_ref[...] = x_ref[...] + _pick_mod(i, n_p, gp_ref, gs_ref) * y


def out_odd_call(dm, x, gate, o_lat_p, o_lat_s, w_comb):
    tm, M, n_p = dm.tm, dm.M, dm.n_p
    sp, ss = dm.mod_specs()
    kdim = H_C * KV_LORA
    return pl.pallas_call(
        functools.partial(_out_odd_kernel, n_p=n_p),
        grid=(dm.n_t,),
        in_specs=[pl.BlockSpec((tm, D_MODEL), lambda i: (i, 0)), sp, ss,
                  pl.BlockSpec((tm, kdim), lambda i: (jnp.minimum(i, n_p - 1), 0)),
                  pl.BlockSpec((tm, kdim), lambda i: (jnp.maximum(i - n_p, 0), 0)),
                  pl.BlockSpec((kdim, D_MODEL), lambda i: (0, 0))],
        out_specs=pl.BlockSpec((tm, D_MODEL), lambda i: (i, 0)),
        out_shape=jax.ShapeDtypeStruct((M, D_MODEL), F32),
        compiler_params=_params(("arbitrary",)),
        name="out_odd",
    )(x, gate[0], gate[1], o_lat_p, o_lat_s, w_comb)


def _moe_pre_kernel(x_ref, g_ref, shp_ref, shs_ref, scp_ref, scs_ref, wr_ref, br_ref, h_ref, gates_ref, *, n_p):
    i = pl.program_id(0)
    shift = _pick_mod(i, n_p, shp_ref, shs_ref)
    scale = _pick_mod(i, n_p, scp_ref, scs_ref)
    h = _rms(x_ref[...], g_ref[...]) * (1.0 + scale) + shift
    h_ref[...] = _bf(h)
    logits = _dot3(h, wr_ref[...]) + br_ref[...]
    ne = logits.shape[1]
    lane = lax.broadcasted_iota(jnp.int32, logits.shape, 1)
    work = logits
    un = jnp.zeros_like(logits)
    den = jnp.zeros((logits.shape[0], 1), F32)
    m0 = None
    for k in range(TOP_K):
        m = jnp.max(work, axis=-1, keepdims=True)
        idx = jnp.min(jnp.where(work == m, lane, ne), axis=-1, keepdims=True)
        hit = lane == idx
        if k == 0:
            m0 = m
        e = jnp.exp(m - m0)
        un = un + jnp.where(hit, e, 0.0)
        den = den + e
        work = jnp.where(hit, -jnp.inf, work)
    gates_ref[...] = un / den


def moe_pre_call(dm, x, g, shift, scale, w_router, b_router):
    tm, M = dm.tm, dm.M
    ne = w_router.shape[1]
    sp, ss = dm.mod_specs()
    return pl.pallas_call(
        functools.partial(_moe_pre_kernel, n_p=dm.n_p),
        grid=(dm.n_t,),
        in_specs=[pl.BlockSpec((tm, D_MODEL), lambda i: (i, 0)), pl.BlockSpec((1, D_MODEL), lambda i: (0, 0)),
                  sp, ss, sp, ss,
                  pl.BlockSpec((D_MODEL, ne), lambda i: (0, 0)), pl.BlockSpec((1, ne), lambda i: (0, 0))],
        out_specs=[pl.BlockSpec((tm, D_MODEL), lambda i: (i, 0)), pl.BlockSpec((tm, ne), lambda i: (i, 0))],
        out_shape=[jax.ShapeDtypeStruct((M, D_MODEL), BF16), jax.ShapeDtypeStruct((M, ne), F32)],
        compiler_params=_params(("arbitrary",)),
        name="moe_router",
    )(x, g, shift[0], shift[1], scale[0], scale[1], w_router, b_router)


def _moe_dense_kernel(h_ref, gates_ref, wgu_ref, bgu_ref, wdn_ref, bdn_ref, x_ref, gp_ref, gs_ref,
                      o_ref, acc_ref, *, n_p, d_ff):
    i = pl.program_id(0)
    e = pl.program_id(1)

    @pl.when(e == 0)
    def _():
        acc_ref[...] = jnp.zeros_like(acc_ref)

    gu = jnp.dot(h_ref[...], wgu_ref[0], preferred_element_type=F32) + bgu_ref[0]
    gate = jnp.minimum(gu[:, :d_ff], SWIGLU_LIMIT)
    up = jnp.clip(gu[:, d_ff:], -SWIGLU_LIMIT, SWIGLU_LIMIT)
    act = (up + 1.0) * gate * jax.nn.sigmoid(SWIGLU_ALPHA * gate)
    y = jnp.dot(_bf(act), wdn_ref[0], preferred_element_type=F32) + bdn_ref[0]
    gates = gates_ref[...]
    lane = lax.broadcasted_iota(jnp.int32, gates.shape, 1)
    g_e = jnp.sum(jnp.where(lane == e, gates, 0.0), axis=-1, keepdims=True)
    acc_ref[...] += g_e * y

    @pl.when(e == pl.num_programs(1) - 1)
    def _():
        o_ref[...] = x_ref[...] + _pick_mod(i, n_p, gp_ref, gs_ref) * acc_ref[...]


def moe_dense_call(dm, h, gates, w_gu, b_gu, w_dn, b_dn, x, gate):
    tm, M = dm.tm, dm.M
    ne, _, n2 = w_gu.shape
    d_ff = n2 // 2
    sp, ss = dm.mod_specs()
    return pl.pallas_call(
        functools.partial(_moe_dense_kernel, n_p=dm.n_p, d_ff=d_ff),
        grid=(dm.n_t, ne),
        in_specs=[pl.BlockSpec((tm, D_MODEL), lambda i, e: (i, 0)),
                  pl.BlockSpec((tm, ne), lambda i, e: (i, 0)),
                  pl.BlockSpec((1, D_MODEL, n2), lambda i, e: (e, 0, 0)),
                  pl.BlockSpec((1, 1, n2), lambda i, e: (e, 0, 0)),
                  pl.BlockSpec((1, d_ff, D_MODEL), lambda i, e: (e, 0, 0)),
                  pl.BlockSpec((1, 1, D_MODEL), lambda i, e: (e, 0, 0)),
                  pl.BlockSpec((tm, D_MODEL), lambda i, e: (i, 0)), sp, ss],
        out_specs=pl.BlockSpec((tm, D_MODEL), lambda i, e: (i, 0)),
        out_shape=jax.ShapeDtypeStruct((M, D_MODEL), F32),
        scratch_shapes=[pltpu.VMEM((tm, D_MODEL), F32)],
        compiler_params=_params(("arbitrary", "arbitrary")),
        name="moe_experts",
    )(h, gates, w_gu, b_gu.reshape(ne, 1, n2), w_dn, b_dn.reshape(ne, 1, D_MODEL), x, gate[0], gate[1])


def _final_norm_kernel(x_ref, g_ref, o_ref):
    o_ref[...] = _rms(x_ref[...], g_ref[...])


def final_norm_call(dm, x, g):
    tm = dm.tm
    return pl.pallas_call(
        _final_norm_kernel,
        grid=(dm.n_t,),
        in_specs=[pl.BlockSpec((tm, D_MODEL), lambda i: (i, 0)), pl.BlockSpec((1, D_MODEL), lambda i: (0, 0))],
        out_specs=pl.BlockSpec((tm, D_MODEL), lambda i: (i, 0)),
        out_shape=jax.ShapeDtypeStruct((dm.M, D_MODEL), F32),
        compiler_params=_params(("arbitrary",)),
        name="final_norm",
    )(x, g)


def _prep_even(w_in, b_f, w_cmp, pe_cmp, w_out):
    a_q, a_kv, a_g, b_q = H_A * HEAD_DIM, KV_W, 3 * H_A, H_B * HEAD_DIM
    o = np.cumsum([0, a_q, a_kv, a_kv, a_kv, a_g, b_q, KV_W, H_B])
    q_a, kv_cmp, kv_sel, kv_win, g_a, q_b, kv_b, f_b = [w_in[:, o[i]:o[i + 1]] for i in range(8)]
    pad = jnp.zeros((D_MODEL, 128 - H_B - a_g), w_in.dtype)
    w_perm = _bf(jnp.concatenate([q_a, q_b, kv_cmp, kv_sel, kv_win, kv_b, f_b, g_a, pad], axis=1))
    bf_row = jnp.concatenate([b_f, jnp.zeros((128 - H_B,), F32)])[None]
    eye = jnp.eye(2, dtype=F32)
    w_big = _bf(jnp.einsum('kldo,kK,gG->lkgdKGo', w_cmp, eye, eye).reshape(L_CMP * KV_W, KV_W))
    pe_flat = jnp.broadcast_to(jnp.transpose(pe_cmp, (1, 0, 2))[:, :, None, :],
                               (L_CMP, 2, G_A, HEAD_DIM)).reshape(1, L_CMP * KV_W)
    w_out_h = _bf(w_out.reshape(H_A + H_B, HEAD_DIM, D_MODEL))
    return w_perm, bf_row, w_big, pe_flat, w_out_h


def _rot_cols(w):
    half = D_ROPE // 2
    return jnp.concatenate([-w[..., half:], w[..., :half]], axis=-1)


def _prep_odd(w_in, w_qb, w_kvb, w_out):
    kpe_w = w_in[:, Q_LORA + KV_LORA:]
    pad = jnp.zeros((D_MODEL, 128 - 2 * D_ROPE), w_in.dtype)
    w_perm = _bf(jnp.concatenate([w_in[:, :Q_LORA + KV_LORA], kpe_w, _rot_cols(kpe_w), pad], axis=1))
    wq = w_qb.reshape(Q_LORA, H_C, D_NOPE + D_ROPE)
    wkv = w_kvb.reshape(KV_LORA, H_C, D_NOPE + D_V)
    nope_h = jnp.transpose(wq[..., :D_NOPE], (1, 0, 2))
    w_uk_h = jnp.transpose(wkv[..., :D_NOPE], (1, 0, 2))
    w_qabs = bmm_call(nope_h, w_uk_h, True, "w_q_absorb")
    w_qabs = _bf(jnp.transpose(w_qabs, (1, 0, 2)).reshape(Q_LORA, H_C * KV_LORA))
    pe_w = wq[..., D_NOPE:]
    w_qpe = _bf(jnp.concatenate([pe_w.reshape(Q_LORA, H_C * D_ROPE),
                                 _rot_cols(pe_w).reshape(Q_LORA, H_C * D_ROPE)], axis=1))
    w_uv_h = jnp.transpose(wkv[..., D_NOPE:], (1, 0, 2))
    w_comb = bmm_call(w_uv_h, w_out.reshape(H_C, D_V, D_MODEL), False, "w_out_absorb")
    w_comb = _bf(w_comb.reshape(H_C * KV_LORA, D_MODEL))
    return w_perm, w_qabs, w_qpe, w_comb


def _rope_tables(dm, past):
    half = D_ROPE // 2
    inv = ROPE_THETA ** (-jnp.arange(half, dtype=F32) / half)
    pos = jnp.concatenate([jnp.tile(jnp.arange(dm.T, dtype=jnp.int32), dm.B),
                           jnp.tile(past + jnp.arange(dm.DT, dtype=jnp.int32), dm.DB)])
    ang = pos.astype(F32)[:, None] * inv[None, :]
    cos32 = jnp.tile(jnp.cos(ang), (1, 2))
    sin32 = jnp.tile(jnp.sin(ang), (1, 2))
    return cos32, sin32, jnp.tile(cos32, (1, H_C)), jnp.tile(sin32, (1, H_C))


def _split3_cols(c):
    hi, mid, lo = _split3(c)
    return [hi, mid, lo]


def _fox_augment(dm, q_hm, kvh, cum):
    Mp, M = dm.Mp, dm.M
    cum = jnp.pad(cum.reshape(Mp, H_B), ((0, M - Mp), (0, 0)))
    ones = jnp.ones((M, 3), BF16)
    zeros3 = jnp.zeros((M, 3), BF16)
    qs, ks = [], []
    for g in range(G_B):
        kcols = [kvh[8 + g]]
        for r in range(R_B):
            kcols += _split3_cols(-cum[:, g * R_B + r:g * R_B + r + 1])
        kcols += [ones, jnp.zeros((M, 128 - HEAD_DIM - 3 * R_B - 3), BF16)]
        ks.append(jnp.concatenate(kcols, axis=1))
        for r in range(R_B):
            h = g * R_B + r
            qcols = [q_hm[H_A + h]] + [ones if rr == r else zeros3 for rr in range(R_B)]
            qcols += _split3_cols(cum[:, h:h + 1])
            qcols += [jnp.zeros((M, 128 - HEAD_DIM - 3 * R_B - 3), BF16)]
            qs.append(jnp.concatenate(qcols, axis=1))
    return jnp.stack(qs), jnp.stack(ks)


NEW_PAD = 8
NEW_LANES = 128


def _page_copy(pool_ref, buf_ref, sem, pt_ref, b, slot, j, rows):
    return pltpu.make_async_copy(pool_ref.at[pt_ref[b, j]], buf_ref.at[slot, pl.ds(j * rows, rows)], sem.at[slot])


def _fetch_pages(pools, pt_ref, n_pages):
    b = pl.program_id(0)
    nb = pl.num_programs(0)
    slot = b % 2

    def start(bb, sl):
        for pool_ref, buf_ref, sem, rows in pools:
            for j in range(n_pages):
                _page_copy(pool_ref, buf_ref, sem, pt_ref, bb, sl, j, rows).start()

    @pl.when(b == 0)
    def _():
        start(b, slot)

    @pl.when(b + 1 < nb)
    def _():
        start(b + 1, 1 - slot)

    for pool_ref, buf_ref, sem, rows in pools:
        for j in range(n_pages):
            _page_copy(pool_ref, buf_ref, sem, pt_ref, b, slot, j, rows).wait()
    return slot


def _softmax2(lg_p, lg_n):
    m = jnp.maximum(jnp.max(lg_p, axis=-1, keepdims=True), jnp.max(lg_n, axis=-1, keepdims=True))
    e_p = jnp.exp(lg_p - m)
    e_n = jnp.exp(lg_n - m)
    s = jnp.sum(e_p, axis=-1, keepdims=True) + jnp.sum(e_n, axis=-1, keepdims=True)
    return e_p / s, e_n / s


def _t5_decode_kernel(tabt_ref, o_ref, *, kind, past, n_past, dt, n_new):
    rows, lanes = o_ref.shape
    r = lax.broadcasted_iota(jnp.int32, (rows, lanes), 0)
    l = lax.broadcasted_iota(jnp.int32, (rows, lanes), 1)
    qpos = past + r // H_A
    u = l - n_past
    if kind == "sel":
        kpos = jnp.where(l < n_past, l, past + u)
        ok = kpos <= qpos
    elif kind == "win":
        kpos = jnp.where(l < n_past, past - n_past + l, past + u)
        ok = (kpos <= qpos) & (qpos - kpos < WINDOW) & (kpos >= 0)
    else:
        kpos = jnp.where(l < n_past, (l + 1) * L_CMP - 1, past + (u + 1) * L_CMP - 1)
        ok = kpos <= qpos
    ok = ok & (u < n_new)
    bucket = _t5_bucket(qpos - kpos)
    acc = jnp.zeros((rows, lanes), F32)
    for b in range(N_BUCKETS):
        acc = jnp.where(bucket == b, tabt_ref[:, b:b + 1], acc)
    o_ref[...] = jnp.where(ok, acc, NEG_BIG)


def t5_decode_call(t5_table, kind, past, n_past, dt, n_new):
    rows = dt * H_A
    tabt = jnp.tile(t5_table.T, (dt, 1))
    return pl.pallas_call(
        functools.partial(_t5_decode_kernel, kind=kind, past=past, n_past=n_past, dt=dt, n_new=n_new),
        out_shape=jax.ShapeDtypeStruct((rows, n_past + NEW_LANES), F32),
        compiler_params=_params(None),
        name="t5_decode_" + kind,
    )(tabt)


def _dec_cmp_kernel(pt_ref, q_ref, pool_ref, cnew_ref, bias_ref, o_ref, sel_ref, buf, sem, *,
                    n_pages, past, dt, n_new, n_sel_blocks, k_sel):
    slot = _fetch_pages([(pool_ref, buf, sem, PAGE_SIZE // L_CMP)], pt_ref, n_pages)
    ncp = buf.shape[1]
    q = q_ref[0]
    c = buf[slot]
    cn = cnew_ref[0]
    lg_p = _dot_nt(q, c[:, :128]) + bias_ref[:, :ncp]
    lg_n = _dot_nt(q, cn[:, :128]) + bias_ref[:, ncp:ncp + NEW_PAD]
    p_p, p_n = _softmax2(lg_p, lg_n)
    o_ref[0] = _dot(p_p, c[:, 128:]) + _dot(p_n, cn[:, 128:])
    n_tg = dt * G_A
    d = functools.partial(jnp.dot, preferred_element_type=F32)
    nsp = sel_ref.shape[2]

    def group_sum(p):
        return jnp.concatenate([jnp.sum(p[i * R_A:(i + 1) * R_A], axis=0, keepdims=True) for i in range(n_tg)],
                               axis=0)

    pair = (lax.broadcasted_iota(jnp.int32, (ncp, nsp), 0) * L_CMP // L_SEL
            == lax.broadcasted_iota(jnp.int32, (ncp, nsp), 1)).astype(BF16)
    hi, mid, lo = _split3(group_sum(p_p))
    imp_s = d(hi, pair) + d(mid, pair) + d(lo, pair)
    blk = lax.broadcasted_iota(jnp.int32, (n_tg, nsp), 1)
    imp_n = group_sum(p_n)
    for u in range(n_new):
        imp_s = imp_s + jnp.where(blk == (past + u * L_CMP) // L_SEL, imp_n[:, u:u + 1], 0.0)
    cur = (past + lax.broadcasted_iota(jnp.int32, (n_tg, nsp), 0) // G_A) // L_SEL
    valid = (blk <= cur) & (blk < n_sel_blocks)
    forced = (blk == 0) | (blk == cur) | (blk == cur - 1)
    score = jnp.where(forced & valid, jnp.inf, jnp.where(valid, imp_s, -jnp.inf))
    chosen = jnp.zeros((n_tg, nsp), jnp.bool_)
    for _ in range(k_sel):
        m = jnp.max(score, axis=-1, keepdims=True)
        idx = jnp.min(jnp.where(score == m, blk, nsp), axis=-1, keepdims=True)
        hit = blk == idx
        chosen = chosen | hit
        score = jnp.where(hit, -jnp.inf, score)
    sel_ref[0] = jnp.where(chosen, 0.0, SEL_NEG).astype(BF16)


def _dec_sel_kernel(pt_ref, q_ref, pool_ref, knew_ref, selneg_ref, bias_ref, o_ref, buf, sem, onehot, *,
                    n_pages, past, dt):
    b = pl.program_id(0)
    nsp, lanes = onehot.shape

    @pl.when(b == 0)
    def _():
        l = lax.broadcasted_iota(jnp.int32, (nsp, lanes), 1)
        onehot[...] = (lax.broadcasted_iota(jnp.int32, (nsp, lanes), 0) == l // L_SEL).astype(BF16)

    slot = _fetch_pages([(pool_ref, buf, sem, PAGE_SIZE)], pt_ref, n_pages)
    q = q_ref[0]
    sel_tg = selneg_ref[0].astype(F32)
    sel_rows = _bf(jnp.concatenate([jnp.broadcast_to(sel_tg[i:i + 1], (R_A, nsp)) for i in range(dt * G_A)],
                                   axis=0))
    sb = jnp.dot(sel_rows, onehot[...], preferred_element_type=F32) + bias_ref[...]
    kv = _bf(buf[slot])
    kn = _bf(knew_ref[0])
    lg_p = _dot_nt(q, kv[:, :128]) + sb[:, :past]
    lg_n = _dot_nt(q, kn[:, :128]) + sb[:, past:past + NEW_PAD]
    p_p, p_n = _softmax2(lg_p, lg_n)
    o_ref[0] = _dot(p_p, kv[:, 128:]) + _dot(p_n, kn[:, 128:])


def _dec_fox_kernel(pt_ref, q_ref, pool_ref, lf_ref, knew_ref, lfn_ref, o_ref, buf, sem, buflf, semlf, *,
                    n_pages, past, dt):
    slot = _fetch_pages([(pool_ref, buf, sem, PAGE_SIZE), (lf_ref, buflf, semlf, H_B)], pt_ref, n_pages)
    q = q_ref[0]
    rows = q.shape[0]
    x = buflf[slot]
    n = x.shape[0]
    tri = (lax.broadcasted_iota(jnp.int32, (PAGE_SIZE, PAGE_SIZE), 0)
           <= lax.broadcasted_iota(jnp.int32, (PAGE_SIZE, PAGE_SIZE), 1)).astype(BF16)
    d = functools.partial(jnp.dot, preferred_element_type=F32)

    def cum_lanes(v):
        hi, mid, lo = _split3(v)
        return d(hi, tri) + d(mid, tri) + d(lo, tri)

    w = cum_lanes(x)
    run = jnp.zeros((H_B, 1), F32)
    pieces = []
    for j in range(n_pages):
        wj = w[j * H_B:(j + 1) * H_B]
        pieces.append(wj + run)
        run = run + wj[:, PAGE_SIZE - 1:PAGE_SIZE]
    ck = jnp.concatenate(pieces, axis=1)
    cnew = cum_lanes(lfn_ref[0]) + run
    ck32 = jnp.concatenate([ck] * dt, axis=0)
    cn32 = jnp.concatenate([cnew[:, :NEW_PAD]] * dt, axis=0)
    cq32 = jnp.concatenate([cnew[:, t:t + 1] for t in range(dt)], axis=0)
    kv = _bf(buf[slot])
    kn = _bf(knew_ref[0])
    lg_p = _dot_nt(q, kv[:, :128]) + (cq32 - ck32)
    u = lax.broadcasted_iota(jnp.int32, (rows, NEW_PAD), 1)
    t = lax.broadcasted_iota(jnp.int32, (rows, NEW_PAD), 0) // H_B
    lg_n = jnp.where(u <= t, _dot_nt(q, kn[:, :128]) + (cq32 - cn32), NEG_BIG)
    p_p, p_n = _softmax2(lg_p, lg_n)
    o_ref[0] = _dot(p_p, kv[:, 128:]) + _dot(p_n, kn[:, 128:])


def _dec_win_kernel(q_ref, win_ref, knew_ref, bias_ref, o_ref):
    q = q_ref[0]
    kv = _bf(win_ref[0])
    kn = _bf(knew_ref[0])
    wb = kv.shape[0]
    lg_p = _dot_nt(q, kv[:, :128]) + bias_ref[:, :wb]
    lg_n = _dot_nt(q, kn[:, :128]) + bias_ref[:, wb:wb + NEW_PAD]
    p_p, p_n = _softmax2(lg_p, lg_n)
    o_ref[0] = _dot(p_p, kv[:, 128:]) + _dot(p_n, kn[:, 128:])


def _dec_mla_kernel(pt_ref, q_ref, qpe_ref, pool_ref, poolpe_ref, cnew_ref, penew_ref, o_ref,
                    buf, sem, bufpe, sempe, *, n_pages):
    slot = _fetch_pages([(pool_ref, buf, sem, PAGE_SIZE), (poolpe_ref, bufpe, sempe, PAGE_SIZE)], pt_ref, n_pages)
    q, qpe = q_ref[...], qpe_ref[...]
    rows = q.shape[0]
    c = _bf(buf[slot])
    cn = _bf(cnew_ref[0])
    lg_p = _dot_nt(q, c) + _dot_nt(qpe, bufpe[slot])
    u = lax.broadcasted_iota(jnp.int32, (rows, NEW_PAD), 1)
    t = lax.broadcasted_iota(jnp.int32, (rows, NEW_PAD), 0) // H_C
    lg_n = jnp.where(u <= t, _dot_nt(q, cn) + _dot_nt(qpe, penew_ref[0]), NEG_BIG)
    p_p, p_n = _softmax2(lg_p, lg_n)
    o_ref[...] = _bf(_dot(p_p, c) + _dot(p_n, cn))


def _paged_call(kern, name, dm, page_table, in_arrays, in_specs, out_shape, out_specs, scratch):
    grid_spec = pltpu.PrefetchScalarGridSpec(
        num_scalar_prefetch=1, grid=(dm.DB,), in_specs=in_specs, out_specs=out_specs, scratch_shapes=scratch)
    return pl.pallas_call(kern, grid_spec=grid_spec, out_shape=out_shape,
                          compiler_params=_params(("arbitrary",)), name=name)(page_table, *in_arrays)


_ANY = pl.BlockSpec(memory_space=pl.ANY)


def _seq_spec(*tail):
    return pl.BlockSpec((1,) + tail, lambda b, *_: (b,) + (0,) * len(tail))


def _const_spec(shape):
    return pl.BlockSpec(shape, lambda b, *_: (0,) * len(shape))


def _q_blockdiag(dm, q_heads):
    DB, DT = dm.DB, dm.DT
    q = jnp.transpose(q_heads.reshape(H_A, DB, DT, HEAD_DIM), (1, 2, 0, 3))
    g0 = (jnp.arange(H_A) < R_A)[None, None, :, None]
    z = jnp.zeros_like(q)
    return jnp.concatenate([jnp.where(g0, q, z), jnp.where(g0, z, q)], axis=-1).reshape(DB, DT * H_A, 2 * HEAD_DIM)


def _from_blockdiag(dm, o):
    DB, DT = dm.DB, dm.DT
    o = o.reshape(DB, DT, H_A, 2, HEAD_DIM)
    g0 = (jnp.arange(H_A) < R_A)[None, None, :, None]
    o = jnp.where(g0, o[..., 0, :], o[..., 1, :])
    return jnp.transpose(o, (2, 0, 1, 3)).reshape(H_A, DB * DT, HEAD_DIM)


def _new_rows(dm, a, width):
    return jnp.pad(a.reshape(dm.DB, dm.DT, width), ((0, 0), (0, NEW_PAD - dm.DT), (0, 0)))


def decode_even(dm, page_table, q_hm, kv_cmp, kv_sel, kv_win, kv_b, logf, pool_cmp, pool_sel, pool_fox_kv,
                pool_fox_logf, win_buf, pe_flat, w_big, dec_bias):
    DB, DT, Mp = dm.DB, dm.DT, dm.Mp
    P = pool_cmp.shape[0]
    n_pages = page_table.shape[1]
    past = n_pages * PAGE_SIZE
    total = past + DT
    padded = -(-total // L_SEL) * L_SEL
    ns = padded // L_SEL
    nsp = -(-ns // 128) * 128
    rows = DT * H_A
    bpp = PAGE_SIZE // L_CMP
    qa = _q_blockdiag(dm, q_hm[:H_A, Mp:])
    qb = _q_blockdiag(dm, q_hm[H_A:, Mp:])
    o_shape = jax.ShapeDtypeStruct((DB, rows, 2 * HEAD_DIM), F32)
    q_spec = _seq_spec(rows, 2 * HEAD_DIM)
    new_spec = _seq_spec(NEW_PAD, KV_W)
    bias_cmp, bias_sel, bias_win = dec_bias

    pool_c, _ = compress_call(pool_cmp.reshape(P * bpp, L_CMP * KV_W), pe_flat, w_big)
    n_new = (padded - past) // L_CMP
    new_rows = jnp.pad(kv_cmp[Mp:].reshape(DB, DT * KV_W), ((0, 0), (0, (padded - total) * KV_W)))
    c_new, _ = compress_call(new_rows.reshape(DB * n_new, L_CMP * KV_W), pe_flat, w_big)
    c_new = jnp.pad(c_new.reshape(DB, n_new, KV_W), ((0, 0), (0, NEW_PAD - n_new), (0, 0)))
    o_cmp, selneg = _paged_call(
        functools.partial(_dec_cmp_kernel, n_pages=n_pages, past=past, dt=DT, n_new=n_new, n_sel_blocks=ns,
                          k_sel=min(N_SEL, ns)),
        "decode_cmp", dm, page_table,
        [qa, pool_c.reshape(P, bpp, KV_W), c_new, bias_cmp],
        [q_spec, _ANY, new_spec, _const_spec(bias_cmp.shape)],
        [o_shape, jax.ShapeDtypeStruct((DB, DT * G_A, nsp), BF16)],
        [q_spec, _seq_spec(DT * G_A, nsp)],
        [pltpu.VMEM((2, n_pages * bpp, KV_W), F32), pltpu.SemaphoreType.DMA((2,))])

    o_sel = _paged_call(
        functools.partial(_dec_sel_kernel, n_pages=n_pages, past=past, dt=DT),
        "decode_sel", dm, page_table,
        [qa, pool_sel.reshape(P, PAGE_SIZE, KV_W), _new_rows(dm, kv_sel[Mp:], KV_W), selneg, bias_sel],
        [q_spec, _ANY, new_spec, _seq_spec(DT * G_A, nsp), _const_spec(bias_sel.shape)],
        o_shape, q_spec,
        [pltpu.VMEM((2, past, KV_W), F32), pltpu.SemaphoreType.DMA((2,)),
         pltpu.VMEM((nsp, past + NEW_LANES), BF16)])

    lf_pool_t = jnp.transpose(pool_fox_logf, (0, 2, 1))
    lf_new_t = jnp.pad(jnp.transpose(logf[Mp:].reshape(DB, DT, H_B), (0, 2, 1)),
                       ((0, 0), (0, 0), (0, PAGE_SIZE - DT)))
    o_fox = _paged_call(
        functools.partial(_dec_fox_kernel, n_pages=n_pages, past=past, dt=DT),
        "decode_fox", dm, page_table,
        [qb, pool_fox_kv.reshape(P, PAGE_SIZE, KV_W), lf_pool_t, _new_rows(dm, kv_b[Mp:], KV_W), lf_new_t],
        [q_spec, _ANY, _ANY, new_spec, _seq_spec(H_B, PAGE_SIZE)],
        o_shape, q_spec,
        [pltpu.VMEM((2, past, KV_W), F32), pltpu.SemaphoreType.DMA((2,)),
         pltpu.VMEM((2, n_pages * H_B, PAGE_SIZE), F32), pltpu.SemaphoreType.DMA((2,))])

    wb = win_buf.shape[1]
    o_win = pl.pallas_call(
        _dec_win_kernel,
        grid=(DB,),
        in_specs=[q_spec, _seq_spec(wb, KV_W), new_spec, _const_spec(bias_win.shape)],
        out_specs=q_spec,
        out_shape=o_shape,
        compiler_params=_params(("arbitrary",)),
        name="decode_win",
    )(qa, win_buf.reshape(DB, wb, KV_W), _new_rows(dm, kv_win[Mp:], KV_W), bias_win)
    return tuple(_from_blockdiag(dm, o) for o in (o_cmp, o_sel, o_win, o_fox))


def decode_odd(dm, page_table, q_lat, q_pe, ckv, kpe_b, pool_ckv, pool_kpe):
    DB, DT, Mp = dm.DB, dm.DT, dm.Mp
    n_pages = page_table.shape[1]
    past = n_pages * PAGE_SIZE
    rows = DT * H_C
    row0 = Mp * H_C // rows
    q_spec = lambda w: pl.BlockSpec((rows, w), lambda b, *_: (row0 + b, 0))
    return _paged_call(
        functools.partial(_dec_mla_kernel, n_pages=n_pages),
        "decode_mla", dm, page_table,
        [q_lat, q_pe, pool_ckv, pool_kpe, _new_rows(dm, ckv[Mp:], KV_LORA), _new_rows(dm, kpe_b[Mp:], D_ROPE)],
        [q_spec(KV_LORA), q_spec(D_ROPE), _ANY, _ANY, _seq_spec(NEW_PAD, KV_LORA), _seq_spec(NEW_PAD, D_ROPE)],
        jax.ShapeDtypeStruct((DB * rows, KV_LORA), BF16),
        pl.BlockSpec((rows, KV_LORA), lambda b, *_: (b, 0)),
        [pltpu.VMEM((2, past, KV_LORA), F32), pltpu.SemaphoreType.DMA((2,)),
         pltpu.VMEM((2, past, D_ROPE), F32), pltpu.SemaphoreType.DMA((2,))])


def kernel(x_prompt, x_sample, c_prompt, c_sample, cache_nsa_cmp, cache_nsa_sel, cache_fox_kv, cache_fox_logf, state_nsa_win, cache_mla_ckv, cache_mla_kpe, page_table, t5_table, w_in_ab, b_forget, w_cmp, pe_cmp, w_out_ab, w_in_c, g_qnorm, w_qb, g_kvnorm, w_kvb, w_out_c, g_norm_mix, g_norm_ffn, w_ada, b_ada, w_router, b_router, w_moe_gu, b_moe_gu, w_moe_down, b_moe_down, g_final):
    B, T, _ = x_prompt.shape
    DB, DT, _ = x_sample.shape
    dm = Dims(B, T, DB, DT)
    Mp, M = dm.Mp, dm.M
    depth = w_ada.shape[0]
    past = page_table.shape[1] * PAGE_SIZE
    kv5 = (2, G_A, HEAD_DIM)

    x = jnp.concatenate([x_prompt.reshape(Mp, D_MODEL), x_sample.reshape(dm.Ms, D_MODEL)], axis=0)
    mod = ada_call(jnp.concatenate([c_prompt, c_sample], axis=0), w_ada, b_ada)

    def mod_vec(layer, k):
        m = mod[layer, :, k * D_MODEL:(k + 1) * D_MODEL]
        return m[:B].reshape(B, 1, D_MODEL), jnp.repeat(m[B:], DT, axis=0)

    strips = t5_strips_call(t5_table)
    cos32, sin32, cos_h, sin_h = _rope_tables(dm, past)
    n_pages = page_table.shape[1]
    wb = state_nsa_win.shape[2]
    n_new_cmp = (-(-(past + DT) // L_SEL) * L_SEL - past) // L_CMP
    dec_bias = (t5_decode_call(t5_table, "cmp", past, n_pages * (PAGE_SIZE // L_CMP), DT, n_new_cmp),
                t5_decode_call(t5_table, "sel", past, past, DT, DT),
                t5_decode_call(t5_table, "win", past, wb, DT, DT))

    even_p, even_s, odd_p, odd_s = [], [], [], []
    for layer in range(depth):
        shift0, scale0, gate0, shift1, scale1, gate1 = [mod_vec(layer, k) for k in range(6)]
        g_mix = g_norm_mix[layer][None]
        if layer % 2 == 0:
            e = layer // 2
            w_perm, bf_row, w_big, pe_flat, w_out_h = _prep_even(w_in_ab[e], b_forget[e], w_cmp[e], pe_cmp[e],
                                                                 w_out_ab[e])
            kv_cmp, kv_sel, kv_win, kv_b, gf, logf, q_hm, kvh = proj_even_call(dm, x, g_mix, shift0, scale0,
                                                                              w_perm, bf_row)
            _, cmp_hm = compress_call(kv_cmp[:Mp].reshape(Mp // L_CMP, L_CMP * KV_W), pe_flat, w_big)
            o_cmp, selneg = cmp_select_prompt_call(dm, t5_table, q_hm, cmp_hm)
            o_sel = flash_gqa_call(dm, "sel", q_hm, 0, kvh, 0, kvh, 2, H_A, selneg=selneg, strips=strips)
            o_win = flash_gqa_call(dm, "win", q_hm, 0, kvh, 4, kvh, 6, H_A, strips=strips)
            logf_t = jnp.transpose(logf[:Mp].reshape(B, T, H_B), (0, 2, 1)).reshape(B * H_B, T)
            cum = jnp.transpose(cumsum_call(logf_t).reshape(B, H_B, T), (0, 2, 1))
            q_fox, k_fox = _fox_augment(dm, q_hm, kvh, cum)
            o_fox = flash_gqa_call(dm, "fox", q_fox, 0, k_fox, 0, kvh, 10, H_B)
            d_cmp, d_sel, d_win, d_fox = decode_even(
                dm, page_table, q_hm, kv_cmp, kv_sel, kv_win, kv_b, logf, cache_nsa_cmp[e], cache_nsa_sel[e],
                cache_fox_kv[e], cache_fox_logf[e], state_nsa_win[e], pe_flat, w_big, dec_bias)
            win_new = jnp.concatenate([state_nsa_win[e], kv_win[Mp:].reshape((DB, DT) + kv5)], axis=1)[:, DT:]
            x = out_even_call(dm, x, gate0, gf, (o_cmp, o_sel, o_win, o_fox), (d_cmp, d_sel, d_win, d_fox), w_out_h)
            n_keep = min(WINDOW, T)
            sp = lambda a: a[:Mp].reshape((B, T) + kv5)
            ss = lambda a: a[Mp:].reshape((DB, DT) + kv5)
            even_p.append((sp(kv_cmp), sp(kv_sel), sp(kv_b), logf[:Mp].reshape(B, T, H_B),
                           sp(kv_win)[:, T - n_keep:]))
            even_s.append((ss(kv_cmp), ss(kv_sel), ss(kv_b), logf[Mp:].reshape(DB, DT, H_B), win_new))
        else:
            o = layer // 2
            w_perm, w_qabs, w_qpe, w_comb = _prep_odd(w_in_c[o], w_qb[o], w_kvb[o], w_out_c[o])
            cqn, ckv, ckv_b, kpe, kpe_b = proj_odd_call(dm, x, g_mix, shift0, scale0, w_perm,
                                                        g_qnorm[o][None], g_kvnorm[o][None], cos32, sin32)
            q_lat, q_pe = q_mla_call(dm, cqn, w_qabs, w_qpe, cos_h, sin_h)
            q_lat = q_lat.reshape(M * H_C, KV_LORA)
            q_pe = q_pe.reshape(M * H_C, D_ROPE)
            o_lat = flash_mla_call(dm, q_lat, q_pe, ckv_b, kpe_b)
            d_lat = decode_odd(dm, page_table, q_lat, q_pe, ckv, kpe_b, cache_mla_ckv[o], cache_mla_kpe[o])
            x = out_odd_call(dm, x, gate0, o_lat.reshape(Mp, H_C * KV_LORA),
                             d_lat.reshape(dm.Ms, H_C * KV_LORA), w_comb)
            odd_p.append((ckv[:Mp].reshape(B, T, KV_LORA), kpe[:Mp].reshape(B, T, D_ROPE)))
            odd_s.append((ckv[Mp:].reshape(DB, DT, KV_LORA), kpe[Mp:].reshape(DB, DT, D_ROPE)))
        h, gates = moe_pre_call(dm, x, g_norm_ffn[layer][None], shift1, scale1, w_router[layer],
                                b_router[layer][None])
        x = moe_dense_call(dm, h, gates, _bf(w_moe_gu[layer]), b_moe_gu[layer], _bf(w_moe_down[layer]),
                           b_moe_down[layer], x, gate1)

    y = final_norm_call(dm, x, g_final[None])
    outs = [y[:Mp].reshape(B, T, D_MODEL), y[Mp:].reshape(DB, DT, D_MODEL)]
    ep = [jnp.stack(a) for a in zip(*even_p)]
    es = [jnp.stack(a) for a in zip(*even_s)]
    op = [jnp.stack(a) for a in zip(*odd_p)]
    os_ = [jnp.stack(a) for a in zip(*odd_s)]
    for a, b in zip(ep, es):
        outs += [a, b]
    for a, b in zip(op, os_):
        outs += [a, b]
    return tuple(outs)
```

```python
import functools
import math

import numpy as np
import jax
import jax.numpy as jnp
from jax import lax
from jax.experimental import pallas as pl
from jax.experimental.pallas import tpu as pltpu

F32 = jnp.float32
BF16 = jnp.bfloat16

D_MODEL = 1024
HEAD_DIM = 64
PAGE_SIZE = 128
H_A, G_A = 8, 2
R_A = H_A // G_A
L_CMP, L_SEL, N_SEL, WINDOW = 32, 64, 16, 512
H_B, G_B = 8, 2
R_B = H_B // G_B
H_C, Q_LORA, KV_LORA, D_NOPE, D_ROPE, D_V = 16, 512, 256, 64, 32, 64
ROPE_THETA = 10000.0
MLA_SCALE = (D_NOPE + D_ROPE) ** -0.5
N_BUCKETS, MAX_DISTANCE = 32, 128
TOP_K = 4
SWIGLU_LIMIT, SWIGLU_ALPHA = 7.0, 1.702
RMS_EPS = 1e-6
KV_W = 2 * G_A * HEAD_DIM

NEG_BIG = -1e30
SEL_NEG = -30000.0
TQ = 128
TK = 512
VMEM_LIMIT = 56 * 1024 * 1024


def _bf(x):
    return x.astype(BF16)


def _dot(a, b):
    return jnp.dot(_bf(a), _bf(b), preferred_element_type=F32)


def _dot_nt(a, b):
    return lax.dot_general(_bf(a), _bf(b), (((1,), (1,)), ((), ())), preferred_element_type=F32)


def _split2(x):
    hi = _bf(x)
    return hi, _bf(x - hi.astype(F32))


def _split3(x):
    hi = _bf(x)
    r = x - hi.astype(F32)
    mid = _bf(r)
    return hi, mid, _bf(r - mid.astype(F32))


def _dot3(a, b):
    ah, al = _split2(a)
    bh, bl = _split2(b)
    d = functools.partial(jnp.dot, preferred_element_type=F32)
    return d(ah, bh) + d(ah, bl) + d(al, bh)


def _dot3_nt(a, b):
    ah, al = _split2(a)
    bh, bl = _split2(b)
    d = functools.partial(lax.dot_general, dimension_numbers=(((1,), (1,)), ((), ())),
                          preferred_element_type=F32)
    return d(ah, bh) + d(ah, bl) + d(al, bh)


def _rms(x, g):
    return x * lax.rsqrt(jnp.mean(x * x, axis=-1, keepdims=True) + RMS_EPS) * g


def _log_sigmoid(z):
    return jnp.minimum(z, 0.0) - jnp.log1p(jnp.exp(-jnp.abs(z)))


def _t5_bucket(dist):
    n = jnp.maximum(dist, 0)
    max_exact = N_BUCKETS // 2
    nf = jnp.maximum(n, 1).astype(F32)
    large = max_exact + (jnp.log(nf / max_exact) / math.log(MAX_DISTANCE / max_exact)
                         * (N_BUCKETS - max_exact)).astype(jnp.int32)
    return jnp.where(n < max_exact, n, jnp.minimum(large, N_BUCKETS - 1))


def _params(sem, vmem=VMEM_LIMIT):
    return pltpu.CompilerParams(dimension_semantics=sem, vmem_limit_bytes=vmem)


class Dims:
    def __init__(self, B, T, DB, DT):
        self.B, self.T, self.DB, self.DT = B, T, DB, DT
        self.Mp, self.Ms = B * T, DB * DT
        self.M = self.Mp + self.Ms
        self.tm = math.gcd(512, self.Ms)
        assert T % self.tm == 0 and T % TK == 0 and self.tm % 8 == 0
        self.n_p = self.Mp // self.tm
        self.n_t = self.M // self.tm
        self.tiles_per_b = T // self.tm

    def mod_specs(self):
        tpb, nb, n_p, tm = self.tiles_per_b, self.B, self.n_p, self.tm
        sp = pl.BlockSpec((1, 1, D_MODEL), lambda i, *_: (jnp.minimum(i // tpb, nb - 1), 0, 0))
        ss = pl.BlockSpec((tm, D_MODEL), lambda i, *_: (jnp.maximum(i - n_p, 0), 0))
        return sp, ss


def _pick_mod(i, n_p, mp_ref, ms_ref):
    return jnp.where(i < n_p, mp_ref[0], ms_ref[...])


def _ada_kernel(c_ref, w_ref, b_ref, o_ref):
    c = c_ref[...]
    o_ref[0] = _dot3(c * jax.nn.sigmoid(c), w_ref[0]) + b_ref[0]


def ada_call(c_all, w_ada, b_ada):
    depth, _, n6 = w_ada.shape
    n_c = c_all.shape[0]
    tn = 1536
    return pl.pallas_call(
        _ada_kernel,
        grid=(depth, n6 // tn),
        in_specs=[pl.BlockSpec((n_c, D_MODEL), lambda l, j: (0, 0)),
                  pl.BlockSpec((1, D_MODEL, tn), lambda l, j: (l, 0, j)),
                  pl.BlockSpec((1, 1, tn), lambda l, j: (l, 0, j))],
        out_specs=pl.BlockSpec((1, n_c, tn), lambda l, j: (l, 0, j)),
        out_shape=jax.ShapeDtypeStruct((depth, n_c, n6), F32),
        compiler_params=_params(("arbitrary", "arbitrary")),
        name="ada_modulation",
    )(c_all, w_ada, b_ada.reshape(depth, 1, n6))


def _t5_strip_kernel(tab_ref, o_ref):
    p = pl.program_id(0)
    i = lax.broadcasted_iota(jnp.int32, (TQ, TK), 0)
    j = lax.broadcasted_iota(jnp.int32, (TQ, TK), 1)
    off = jnp.where(p < 4, p * TQ, jnp.where(p == 4, TK, 4 * TK))
    bucket = _t5_bucket(off + i - j)
    for h in range(H_A):
        acc = jnp.zeros((TQ, TK), F32)
        for b in range(N_BUCKETS):
            acc = jnp.where(bucket == b, tab_ref[b, h], acc)
        o_ref[h, 0] = acc


def t5_strips_call(t5_table):
    return pl.pallas_call(
        _t5_strip_kernel,
        grid=(6,),
        in_specs=[pl.BlockSpec(memory_space=pltpu.SMEM)],
        out_specs=pl.BlockSpec((H_A, 1, TQ, TK), lambda p: (0, p, 0, 0)),
        out_shape=jax.ShapeDtypeStruct((H_A, 6, TQ, TK), F32),
        compiler_params=_params(("arbitrary",)),
        name="t5_strips",
    )(t5_table)


def _proj_even_kernel(x_ref, g_ref, shp_ref, shs_ref, scp_ref, scs_ref, w_ref, bf_ref,
                      cmp_ref, sel_ref, win_ref, kvb_ref, gf_ref, logf_ref, q_ref, kvh_ref, *, n_p):
    i = pl.program_id(0)
    shift = _pick_mod(i, n_p, shp_ref, shs_ref)
    scale = _pick_mod(i, n_p, scp_ref, scs_ref)
    hb = _bf(_rms(x_ref[...], g_ref[...]) * (1.0 + scale) + shift)
    q = jnp.dot(hb, w_ref[:, 0:1024], preferred_element_type=F32) * (HEAD_DIM ** -0.5)
    for h in range(H_A + H_B):
        q_ref[h] = _bf(q[:, h * HEAD_DIM:(h + 1) * HEAD_DIM])
    kv = jnp.dot(hb, w_ref[:, 1024:2048], preferred_element_type=F32)
    cmp_ref[...] = kv[:, 0:256]
    sel_ref[...] = kv[:, 256:512]
    win_ref[...] = kv[:, 512:768]
    kvb_ref[...] = kv[:, 768:1024]
    for c in range(12):
        kvh_ref[c] = _bf(kv[:, 256 + c * HEAD_DIM:256 + (c + 1) * HEAD_DIM])
    gfr = jnp.dot(hb, w_ref[:, 2048:2176], preferred_element_type=F32)
    lane = lax.broadcasted_iota(jnp.int32, gfr.shape, 1)
    lsg = _log_sigmoid(gfr + bf_ref[...])
    gf_ref[...] = jnp.where(lane < H_B, lsg, jax.nn.sigmoid(gfr))
    logf_ref[...] = lsg[:, 0:H_B]


def proj_even_call(dm, x, g, shift, scale, w_perm, bf_row):
    tm, M = dm.tm, dm.M
    sp, ss = dm.mod_specs()
    row = lambda w: pl.BlockSpec((tm, w), lambda i: (i, 0))
    outs = [jax.ShapeDtypeStruct((M, KV_W), F32)] * 4 + [
        jax.ShapeDtypeStruct((M, 128), F32), jax.ShapeDtypeStruct((M, H_B), F32),
        jax.ShapeDtypeStruct((H_A + H_B, M, HEAD_DIM), BF16),
        jax.ShapeDtypeStruct((12, M, HEAD_DIM), BF16)]
    return pl.pallas_call(
        functools.partial(_proj_even_kernel, n_p=dm.n_p),
        grid=(dm.n_t,),
        in_specs=[row(D_MODEL), pl.BlockSpec((1, D_MODEL), lambda i: (0, 0)), sp, ss, sp, ss,
                  pl.BlockSpec((D_MODEL, 2176), lambda i: (0, 0)),
                  pl.BlockSpec((1, 128), lambda i: (0, 0))],
        out_specs=[row(KV_W)] * 4 + [row(128), row(H_B),
                   pl.BlockSpec((H_A + H_B, tm, HEAD_DIM), lambda i: (0, i, 0)),
                   pl.BlockSpec((12, tm, HEAD_DIM), lambda i: (0, i, 0))],
        out_shape=outs,
        compiler_params=_params(("arbitrary",)),
        name="proj_even",
    )(x, g, shift[0], shift[1], scale[0], scale[1], w_perm, bf_row)


def _compress_kernel(xk_ref, xv_ref, pe_ref, w_ref, o_ref, oh_ref):
    tb = o_ref.shape[0]
    hw = KV_W // 2
    yk = jnp.zeros((tb, hw), F32)
    yv = jnp.zeros((tb, hw), F32)
    for l in range(L_CMP):
        yk = yk + _dot(xk_ref[pl.ds(l, tb, stride=L_CMP), :] + pe_ref[l:l + 1, :hw], w_ref[l, :hw, :hw])
        yv = yv + _dot(xv_ref[pl.ds(l, tb, stride=L_CMP), :] + pe_ref[l:l + 1, hw:], w_ref[l, hw:, hw:])
    y = jnp.concatenate([yk, yv], axis=1)
    o_ref[...] = y
    for c in range(4):
        oh_ref[c] = _bf(y[:, c * HEAD_DIM:(c + 1) * HEAD_DIM])


def compress_call(rows, pe_flat, w_big):
    n_blk = rows.shape[0] // L_CMP
    tb = math.gcd(n_blk, 256)
    return pl.pallas_call(
        _compress_kernel,
        grid=(n_blk // tb,),
        in_specs=[pl.BlockSpec((tb * L_CMP, KV_W // 2), lambda i: (i, 0)),
                  pl.BlockSpec((tb * L_CMP, KV_W // 2), lambda i: (i, 1)),
                  pl.BlockSpec((L_CMP, KV_W), lambda i: (0, 0)),
                  pl.BlockSpec((L_CMP, KV_W, KV_W), lambda i: (0, 0, 0))],
        out_specs=[pl.BlockSpec((tb, KV_W), lambda i: (i, 0)),
                   pl.BlockSpec((4, tb, HEAD_DIM), lambda i: (0, i, 0))],
        out_shape=[jax.ShapeDtypeStruct((n_blk, KV_W), F32),
                   jax.ShapeDtypeStruct((4, n_blk, HEAD_DIM), BF16)],
        compiler_params=_params(("arbitrary",)),
        name="compress",
    )(rows, rows, pe_flat.reshape(L_CMP, KV_W), w_big.reshape(L_CMP, KV_W, KV_W))


def _cumsum_kernel(x_ref, o_ref, carry_ref):
    j = pl.program_id(0)

    @pl.when(j == 0)
    def _():
        carry_ref[...] = jnp.zeros_like(carry_ref)

    n = x_ref.shape[1]
    u = lax.broadcasted_iota(jnp.int32, (n, n), 0)
    s = lax.broadcasted_iota(jnp.int32, (n, n), 1)
    tri = (u <= s).astype(BF16)
    hi, mid, lo = _split3(x_ref[...])
    d = functools.partial(jnp.dot, preferred_element_type=F32)
    c = d(hi, tri) + d(mid, tri) + d(lo, tri) + carry_ref[...]
    o_ref[...] = c
    carry_ref[...] = c[:, n - 1:n]


def cumsum_call(x):
    rows, n = x.shape
    tn = math.gcd(n, 512)
    return pl.pallas_call(
        _cumsum_kernel,
        grid=(n // tn,),
        in_specs=[pl.BlockSpec((rows, tn), lambda j: (0, j))],
        out_specs=pl.BlockSpec((rows, tn), lambda j: (0, j)),
        out_shape=jax.ShapeDtypeStruct((rows, n), F32),
        scratch_shapes=[pltpu.VMEM((rows, 1), F32)],
        compiler_params=_params(("arbitrary",)),
        name="cumsum_logf",
    )(x)


def _cmp_select_kernel(tab_ref, q_ref, ck_ref, cv_ref, o_ref, sel_ref, *, n_sel_blocks, k_sel, pos0, pos_mod):
    g = pl.program_id(0) % G_A
    qi = pl.program_id(1)
    tq = q_ref.shape[1]
    nc = ck_ref.shape[1]
    row = lax.broadcasted_iota(jnp.int32, (tq, nc), 0)
    if pos_mod is None:
        tpos = pos0 + qi * tq + row
    else:
        tpos = pos0 + row % pos_mod
    c_end = (lax.broadcasted_iota(jnp.int32, (tq, nc), 1) + 1) * L_CMP - 1
    vis = c_end <= tpos
    bucket = _t5_bucket(tpos - c_end)
    q = q_ref[...]
    lg = _dot_nt(q.reshape(R_A * tq, HEAD_DIM), ck_ref[0]).reshape(R_A, tq, nc)
    imp = jnp.zeros((tq, nc), F32)
    for r in range(R_A):
        bias = jnp.zeros((tq, nc), F32)
        for b in range(N_BUCKETS):
            bias = jnp.where(bucket == b, tab_ref[b, g * R_A + r], bias)
        lr = jnp.where(vis, lg[r] + bias, NEG_BIG)
        m = jnp.max(lr, axis=-1, keepdims=True)
        e = jnp.where(vis, jnp.exp(lr - m), 0.0)
        s = jnp.sum(e, axis=-1, keepdims=True)
        p = e / jnp.where(s > 0, s, 1.0)
        o_ref[r] = _dot(p, cv_ref[0])
        imp = imp + p
    nsp = sel_ref.shape[2]
    cpb = L_SEL // L_CMP
    pr = lax.broadcasted_iota(jnp.int32, (nc, nsp), 0) // cpb
    pc = lax.broadcasted_iota(jnp.int32, (nc, nsp), 1)
    pair = (pr == pc).astype(BF16)
    hi, mid, lo = _split3(imp)
    d = functools.partial(jnp.dot, preferred_element_type=F32)
    imp_s = d(hi, pair) + d(mid, pair) + d(lo, pair)
    blk = lax.broadcasted_iota(jnp.int32, (tq, nsp), 1)
    cur = tpos[:, 0:1] // L_SEL
    valid = (blk <= cur) & (blk < n_sel_blocks)
    forced = (blk == 0) | (blk == cur) | (blk == cur - 1)
    score = jnp.where(forced & valid, jnp.inf, jnp.where(valid, imp_s, -jnp.inf))
    chosen = jnp.zeros((tq, nsp), jnp.bool_)
    for _ in range(k_sel):
        m = jnp.max(score, axis=-1, keepdims=True)
        idx = jnp.min(jnp.where(score == m, blk, nsp), axis=-1, keepdims=True)
        hit = blk == idx
        chosen = chosen | hit
        score = jnp.where(hit, -jnp.inf, score)
    sel_ref[0] = jnp.where(chosen, 0.0, SEL_NEG).astype(BF16)


def cmp_select_prompt_call(dm, t5_table, q_hm, cmp_hm):
    B, T, M = dm.B, dm.T, dm.M
    nq, nc = T // TQ, T // L_CMP
    ns = T // L_SEL
    nsp = max(128, ns)
    kern = functools.partial(_cmp_select_kernel, n_sel_blocks=ns, k_sel=min(N_SEL, ns), pos0=0, pos_mod=None)
    return pl.pallas_call(
        kern,
        grid=(B * G_A, nq),
        in_specs=[pl.BlockSpec(memory_space=pltpu.SMEM),
                  pl.BlockSpec((R_A, TQ, HEAD_DIM), lambda bg, qi: (bg % G_A, (bg // G_A) * nq + qi, 0)),
                  pl.BlockSpec((1, nc, HEAD_DIM), lambda bg, qi: (bg % G_A, bg // G_A, 0)),
                  pl.BlockSpec((1, nc, HEAD_DIM), lambda bg, qi: (G_A + bg % G_A, bg // G_A, 0))],
        out_specs=[pl.BlockSpec((R_A, TQ, HEAD_DIM), lambda bg, qi: (bg % G_A, (bg // G_A) * nq + qi, 0)),
                   pl.BlockSpec((1, TQ, nsp), lambda bg, qi: (bg % G_A, (bg // G_A) * nq + qi, 0))],
        out_shape=[jax.ShapeDtypeStruct((H_A, dm.Mp, HEAD_DIM), F32),
                   jax.ShapeDtypeStruct((G_A, dm.Mp, nsp), BF16)],
        compiler_params=_params(("arbitrary", "arbitrary")),
        name="nsa_cmp_select_prompt",
    )(t5_table, q_hm, cmp_hm, cmp_hm)


def _steps(T, window):
    qt, kt, ft = [], [], []
    for qi in range(T // TQ):
        kd = (qi * TQ) // TK
        p = (qi * TQ % TK) // TQ
        k0 = max(kd - 1, 0) if window else 0
        for ki in range(k0, kd + 1):
            strip = p if ki == kd else (4 if (ki == kd - 1 and p == 0) else 5)
            fl = (1 if ki == k0 else 0) | (2 if ki == kd else 0) | (4 if ki == kd else 0) | (strip << 4)
            qt.append(qi), kt.append(ki), ft.append(fl)
    return (np.asarray(qt, np.int32), np.asarray(kt, np.int32), np.asarray(ft, np.int32))


def _flash_kernel(qt_ref, kt_ref, ft_ref, *refs, mode, rows_per_tok):
    if mode == "mla":
        q_ref, qpe_ref, k_ref, kpe_ref, o_ref, m_s, l_s, acc_s = refs
    elif mode == "sel":
        q_ref, k_ref, v_ref, selneg_ref, strip_ref, o_ref, m_s, l_s, acc_s = refs
    elif mode == "win":
        q_ref, k_ref, v_ref, strip_ref, o_ref, m_s, l_s, acc_s = refs
    else:
        q_ref, k_ref, v_ref, o_ref, m_s, l_s, acc_s = refs
    s = pl.program_id(1)
    qi, ki, fl = qt_ref[s], kt_ref[s], ft_ref[s]
    rows = m_s.shape[0]
    tk = k_ref.shape[-2]

    @pl.when((fl & 1) != 0)
    def _init():
        m_s[...] = jnp.full_like(m_s, NEG_BIG)
        l_s[...] = jnp.zeros_like(l_s)
        acc_s[...] = jnp.zeros_like(acc_s)

    def step(masked):
        if mode == "mla":
            kb = k_ref[...]
            lg = _dot_nt(q_ref[...], kb) + _dot_nt(qpe_ref[...], kpe_ref[...])
            vb = kb
        else:
            q = q_ref[...]
            lg = _dot_nt(q.reshape(rows, q.shape[-1]), k_ref[0])
            vb = v_ref[0]
        if mode == "sel":
            nsp = selneg_ref.shape[2]
            blk = lax.broadcasted_iota(jnp.int32, (nsp, tk), 0)
            key = ki * tk + lax.broadcasted_iota(jnp.int32, (nsp, tk), 1)
            onehot = (blk == key // L_SEL).astype(BF16)
            sb = jnp.dot(selneg_ref[0], onehot, preferred_element_type=F32)
            lg = (lg.reshape(R_A, TQ, tk) + sb[None]).reshape(rows, tk)
        if mode in ("sel", "win"):
            lg = lg + strip_ref[:, 0].reshape(rows, tk)
        if masked:
            r = lax.broadcasted_iota(jnp.int32, (rows, tk), 0)
            tok = r // rows_per_tok if rows_per_tok > 1 else r % TQ
            dist = (qi * TQ + tok) - (ki * tk + lax.broadcasted_iota(jnp.int32, (rows, tk), 1))
            ok = dist >= 0
            if mode == "win":
                ok = ok & (dist < WINDOW)
            lg = jnp.where(ok, lg, NEG_BIG)
        m_prev = m_s[...]
        m_new = jnp.maximum(m_prev, jnp.max(lg, axis=-1, keepdims=True))
        alpha = jnp.exp(m_prev - m_new)
        p = jnp.exp(lg - m_new)
        l_s[...] = alpha * l_s[...] + jnp.sum(p, axis=-1, keepdims=True)
        acc_s[...] = alpha * acc_s[...] + _dot(p, vb)
        m_s[...] = m_new

    if mode == "win":
        step(True)
    else:
        @pl.when((fl & 4) != 0)
        def _diag():
            step(True)

        @pl.when((fl & 4) == 0)
        def _inner():
            step(False)

    @pl.when((fl & 2) != 0)
    def _fin():
        o = acc_s[...] / l_s[...]
        o_ref[...] = o.reshape(o_ref.shape).astype(o_ref.dtype)


def _flash_scratch(rows, dv):
    return [pltpu.VMEM((rows, 1), F32), pltpu.VMEM((rows, 1), F32), pltpu.VMEM((rows, dv), F32)]


def flash_gqa_call(dm, mode, q, q_head0, k, k_idx0, v, v_idx0, n_heads_out, selneg=None, strips=None):
    B, T, M = dm.B, dm.T, dm.M
    R, G = R_A, G_A
    nq, nk = T // TQ, T // TK
    qt, kt, ft = _steps(T, window=(mode == "win"))
    dq, dk, dv = q.shape[-1], k.shape[-1], v.shape[-1]
    qh0 = q_head0 // R
    in_specs = [
        pl.BlockSpec((R, TQ, dq), lambda bg, s, qt, kt, ft: (qh0 + bg % G, (bg // G) * nq + qt[s], 0)),
        pl.BlockSpec((1, TK, dk), lambda bg, s, qt, kt, ft: (k_idx0 + bg % G, (bg // G) * nk + kt[s], 0)),
        pl.BlockSpec((1, TK, dv), lambda bg, s, qt, kt, ft: (v_idx0 + bg % G, (bg // G) * nk + kt[s], 0)),
    ]
    args = [q, k, v]
    if mode == "sel":
        nsp = selneg.shape[-1]
        in_specs.append(pl.BlockSpec((1, TQ, nsp), lambda bg, s, qt, kt, ft: (bg % G, (bg // G) * nq + qt[s], 0)))
        args.append(selneg)
    if mode in ("sel", "win"):
        in_specs.append(pl.BlockSpec((R, 1, TQ, TK), lambda bg, s, qt, kt, ft: (bg % G, ft[s] >> 4, 0, 0)))
        args.append(strips)
    grid_spec = pltpu.PrefetchScalarGridSpec(
        num_scalar_prefetch=3,
        grid=(B * G, len(qt)),
        in_specs=in_specs,
        out_specs=pl.BlockSpec((R, TQ, dv), lambda bg, s, qt, kt, ft: (bg % G, (bg // G) * nq + qt[s], 0)),
        scratch_shapes=_flash_scratch(R * TQ, dv),
    )
    return pl.pallas_call(
        functools.partial(_flash_kernel, mode=mode, rows_per_tok=1),
        grid_spec=grid_spec,
        out_shape=jax.ShapeDtypeStruct((n_heads_out, dm.Mp, dv), F32),
        compiler_params=_params(("arbitrary", "arbitrary")),
        name="flash_" + mode,
    )(jnp.asarray(qt), jnp.asarray(kt), jnp.asarray(ft), *args)


def flash_mla_call(dm, q_lat, q_pe, ckv, kpe):
    B, T, M = dm.B, dm.T, dm.M
    nq, nk = T // TQ, T // TK
    qt, kt, ft = _steps(T, window=False)
    rows = TQ * H_C
    grid_spec = pltpu.PrefetchScalarGridSpec(
        num_scalar_prefetch=3,
        grid=(B, len(qt)),
        in_specs=[
            pl.BlockSpec((rows, KV_LORA), lambda b, s, qt, kt, ft: (b * nq + qt[s], 0)),
            pl.BlockSpec((rows, D_ROPE), lambda b, s, qt, kt, ft: (b * nq + qt[s], 0)),
            pl.BlockSpec((TK, KV_LORA), lambda b, s, qt, kt, ft: (b * nk + kt[s], 0)),
            pl.BlockSpec((TK, D_ROPE), lambda b, s, qt, kt, ft: (b * nk + kt[s], 0)),
        ],
        out_specs=pl.BlockSpec((rows, KV_LORA), lambda b, s, qt, kt, ft: (b * nq + qt[s], 0)),
        scratch_shapes=_flash_scratch(rows, KV_LORA),
    )
    return pl.pallas_call(
        functools.partial(_flash_kernel, mode="mla", rows_per_tok=H_C),
        grid_spec=grid_spec,
        out_shape=jax.ShapeDtypeStruct((dm.Mp * H_C, KV_LORA), BF16),
        compiler_params=_params(("arbitrary", "arbitrary")),
        name="flash_mla",
    )(jnp.asarray(qt), jnp.asarray(kt), jnp.asarray(ft), q_lat, q_pe, ckv, kpe)


def _out_even_kernel(x_ref, gp_ref, gs_ref, gf_ref, *refs, n_p):
    w_ref, o_ref = refs[8], refs[9]
    i = pl.program_id(0)
    oc, os_, ow, ob = [lambda h, p=refs[k], s=refs[4 + k]: jnp.where(i < n_p, p[h], s[h]) for k in range(4)]
    gf = gf_ref[...]
    acc = jnp.zeros(x_ref.shape, F32)
    for h in range(H_A):
        a = (gf[:, H_B + h:H_B + h + 1] * oc(h)
             + gf[:, H_B + H_A + h:H_B + H_A + h + 1] * os_(h)
             + gf[:, H_B + 2 * H_A + h:H_B + 2 * H_A + h + 1] * ow(h))
        acc = acc + _dot(a, w_ref[h])
    for h in range(H_B):
        acc = acc + _dot(ob(h), w_ref[H_A + h])
    o_ref[...] = x_ref[...] + _pick_mod(i, n_p, gp_ref, gs_ref) * acc


def out_even_call(dm, x, gate, gf, o_prompt, o_sample, w_out_h):
    tm, M, n_p = dm.tm, dm.M, dm.n_p
    sp, ss = dm.mod_specs()
    hm_p = pl.BlockSpec((H_A, tm, HEAD_DIM), lambda i: (0, jnp.minimum(i, n_p - 1), 0))
    hm_s = pl.BlockSpec((H_A, tm, HEAD_DIM), lambda i: (0, jnp.maximum(i - n_p, 0), 0))
    return pl.pallas_call(
        functools.partial(_out_even_kernel, n_p=n_p),
        grid=(dm.n_t,),
        in_specs=[pl.BlockSpec((tm, D_MODEL), lambda i: (i, 0)), sp, ss,
                  pl.BlockSpec((tm, 128), lambda i: (i, 0))] + [hm_p] * 4 + [hm_s] * 4 + [
                  pl.BlockSpec((H_A + H_B, HEAD_DIM, D_MODEL), lambda i: (0, 0, 0))],
        out_specs=pl.BlockSpec((tm, D_MODEL), lambda i: (i, 0)),
        out_shape=jax.ShapeDtypeStruct((M, D_MODEL), F32),
        compiler_params=_params(("arbitrary",)),
        name="out_even",
    )(x, gate[0], gate[1], gf, *o_prompt, *o_sample, w_out_h)


def _proj_odd_kernel(x_ref, g_ref, shp_ref, shs_ref, scp_ref, scs_ref, w_ref, gq_ref, gkv_ref, cos_ref, sin_ref,
                     cqn_ref, ckv_ref, ckvb_ref, kpe_ref, kpeb_ref, *, n_p):
    i = pl.program_id(0)
    shift = _pick_mod(i, n_p, shp_ref, shs_ref)
    scale = _pick_mod(i, n_p, scp_ref, scs_ref)
    hb = _bf(_rms(x_ref[...], g_ref[...]) * (1.0 + scale) + shift)
    cq = jnp.dot(hb, w_ref[:, 0:Q_LORA], preferred_element_type=F32)
    cqn_ref[...] = _bf(_rms(cq, gq_ref[...]))
    ckv = _rms(jnp.dot(hb, w_ref[:, Q_LORA:Q_LORA + KV_LORA], preferred_element_type=F32), gkv_ref[...])
    ckv_ref[...] = ckv
    ckvb_ref[...] = _bf(ckv)
    pe = jnp.dot(hb, w_ref[:, Q_LORA + KV_LORA:Q_LORA + KV_LORA + 128], preferred_element_type=F32)
    kpe = pe[:, 0:D_ROPE] * cos_ref[...] + pe[:, D_ROPE:2 * D_ROPE] * sin_ref[...]
    kpe_ref[...] = kpe
    kpeb_ref[...] = _bf(kpe)


def proj_odd_call(dm, x, g, shift, scale, w_perm, g_q, g_kv, cos32, sin32):
    tm, M = dm.tm, dm.M
    sp, ss = dm.mod_specs()
    row = lambda w: pl.BlockSpec((tm, w), lambda i: (i, 0))
    const = lambda r, c: pl.BlockSpec((r, c), lambda i: (0, 0))
    return pl.pallas_call(
        functools.partial(_proj_odd_kernel, n_p=dm.n_p),
        grid=(dm.n_t,),
        in_specs=[row(D_MODEL), const(1, D_MODEL), sp, ss, sp, ss, const(D_MODEL, 896),
                  const(1, Q_LORA), const(1, KV_LORA), row(D_ROPE), row(D_ROPE)],
        out_specs=[row(Q_LORA), row(KV_LORA), row(KV_LORA), row(D_ROPE), row(D_ROPE)],
        out_shape=[jax.ShapeDtypeStruct((M, Q_LORA), BF16), jax.ShapeDtypeStruct((M, KV_LORA), F32),
                   jax.ShapeDtypeStruct((M, KV_LORA), BF16), jax.ShapeDtypeStruct((M, D_ROPE), F32),
                   jax.ShapeDtypeStruct((M, D_ROPE), BF16)],
        compiler_params=_params(("arbitrary",)),
        name="proj_odd",
    )(x, g, shift[0], shift[1], scale[0], scale[1], w_perm, g_q, g_kv, cos32, sin32)


def _bmm_nt_kernel(a_ref, b_ref, o_ref):
    o_ref[0] = _dot3_nt(a_ref[0], b_ref[0])


def _bmm_kernel(a_ref, b_ref, o_ref):
    o_ref[0] = _dot3(a_ref[0], b_ref[0])


def bmm_call(a, b, nt, name):
    H, m, _ = a.shape
    n = b.shape[1] if nt else b.shape[2]
    return pl.pallas_call(
        _bmm_nt_kernel if nt else _bmm_kernel,
        grid=(H,),
        in_specs=[pl.BlockSpec((1,) + a.shape[1:], lambda h: (h, 0, 0)),
                  pl.BlockSpec((1,) + b.shape[1:], lambda h: (h, 0, 0))],
        out_specs=pl.BlockSpec((1, m, n), lambda h: (h, 0, 0)),
        out_shape=jax.ShapeDtypeStruct((H, m, n), F32),
        compiler_params=_params(("arbitrary",)),
        name=name,
    )(a, b)


def _qlat_kernel(a_ref, w_ref, o_ref):
    o_ref[...] = _bf(jnp.dot(a_ref[...], w_ref[...], preferred_element_type=F32) * MLA_SCALE)


def _qpe_kernel(a_ref, w_ref, cos_ref, sin_ref, o_ref):
    y = jnp.dot(a_ref[...], w_ref[...], preferred_element_type=F32)
    n = o_ref.shape[1]
    o_ref[...] = _bf((y[:, 0:n] * cos_ref[...] + y[:, n:2 * n] * sin_ref[...]) * MLA_SCALE)


def q_mla_call(dm, cqn, w_qabs, w_qpe, cos_h, sin_h):
    tm, M = dm.tm, dm.M
    n_lat = H_C * KV_LORA
    n_pe = H_C * D_ROPE
    tn = 1024
    q_lat = pl.pallas_call(
        _qlat_kernel,
        grid=(dm.n_t, n_lat // tn),
        in_specs=[pl.BlockSpec((tm, Q_LORA), lambda i, j: (i, 0)),
                  pl.BlockSpec((Q_LORA, tn), lambda i, j: (0, j))],
        out_specs=pl.BlockSpec((tm, tn), lambda i, j: (i, j)),
        out_shape=jax.ShapeDtypeStruct((M, n_lat), BF16),
        compiler_params=_params(("arbitrary", "arbitrary")),
        name="q_lat",
    )(cqn, w_qabs)
    q_pe = pl.pallas_call(
        _qpe_kernel,
        grid=(dm.n_t,),
        in_specs=[pl.BlockSpec((tm, Q_LORA), lambda i: (i, 0)),
                  pl.BlockSpec((Q_LORA, 2 * n_pe), lambda i: (0, 0)),
                  pl.BlockSpec((tm, n_pe), lambda i: (i, 0)),
                  pl.BlockSpec((tm, n_pe), lambda i: (i, 0))],
        out_specs=pl.BlockSpec((tm, n_pe), lambda i: (i, 0)),
        out_shape=jax.ShapeDtypeStruct((M, n_pe), BF16),
        compiler_params=_params(("arbitrary",)),
        name="q_pe",
    )(cqn, w_qpe, cos_h, sin_h)
    return q_lat, q_pe


def _out_odd_kernel(x_ref, gp_ref, gs_ref, op_ref, os_ref, w_ref, o_ref, *, n_p):
    i = pl.program_id(0)
    o_lat = jnp.where(i < n_p, op_ref[...], os_ref[...])
    y = jnp.dot(o_lat, w_ref[...], preferred_element_type=F32)
    o_ref[...] = x_ref[...] + _pick_mod(i, n_p, gp_ref, gs_ref) * y


def out_odd_call(dm, x, gate, o_lat_p, o_lat_s, w_comb):
    tm, M, n_p = dm.tm, dm.M, dm.n_p
    sp, ss = dm.mod_specs()
    kdim = H_C * KV_LORA
    return pl.pallas_call(
        functools.partial(_out_odd_kernel, n_p=n_p),
        grid=(dm.n_t,),
        in_specs=[pl.BlockSpec((tm, D_MODEL), lambda i: (i, 0)), sp, ss,
                  pl.BlockSpec((tm, kdim), lambda i: (jnp.minimum(i, n_p - 1), 0)),
                  pl.BlockSpec((tm, kdim), lambda i: (jnp.maximum(i - n_p, 0), 0)),
                  pl.BlockSpec((kdim, D_MODEL), lambda i: (0, 0))],
        out_specs=pl.BlockSpec((tm, D_MODEL), lambda i: (i, 0)),
        out_shape=jax.ShapeDtypeStruct((M, D_MODEL), F32),
        compiler_params=_params(("arbitrary",)),
        name="out_odd",
    )(x, gate[0], gate[1], o_lat_p, o_lat_s, w_comb)


def _moe_pre_kernel(x_ref, g_ref, shp_ref, shs_ref, scp_ref, scs_ref, wr_ref, br_ref, h_ref, gates_ref, *, n_p):
    i = pl.program_id(0)
    shift = _pick_mod(i, n_p, shp_ref, shs_ref)
    scale = _pick_mod(i, n_p, scp_ref, scs_ref)
    h = _rms(x_ref[...], g_ref[...]) * (1.0 + scale) + shift
    h_ref[...] = _bf(h)
    logits = _dot3(h, wr_ref[...]) + br_ref[...]
    ne = logits.shape[1]
    lane = lax.broadcasted_iota(jnp.int32, logits.shape, 1)
    work = logits
    un = jnp.zeros_like(logits)
    den = jnp.zeros((logits.shape[0], 1), F32)
    m0 = None
    for k in range(TOP_K):
        m = jnp.max(work, axis=-1, keepdims=True)
        idx = jnp.min(jnp.where(work == m, lane, ne), axis=-1, keepdims=True)
        hit = lane == idx
        if k == 0:
            m0 = m
        e = jnp.exp(m - m0)
        un = un + jnp.where(hit, e, 0.0)
        den = den + e
        work = jnp.where(hit, -jnp.inf, work)
    gates_ref[...] = un / den


def moe_pre_call(dm, x, g, shift, scale, w_router, b_router):
    tm, M = dm.tm, dm.M
    ne = w_router.shape[1]
    sp, ss = dm.mod_specs()
    return pl.pallas_call(
        functools.partial(_moe_pre_kernel, n_p=dm.n_p),
        grid=(dm.n_t,),
        in_specs=[pl.BlockSpec((tm, D_MODEL), lambda i: (i, 0)), pl.BlockSpec((1, D_MODEL), lambda i: (0, 0)),
                  sp, ss, sp, ss,
                  pl.BlockSpec((D_MODEL, ne), lambda i: (0, 0)), pl.BlockSpec((1, ne), lambda i: (0, 0))],
        out_specs=[pl.BlockSpec((tm, D_MODEL), lambda i: (i, 0)), pl.BlockSpec((tm, ne), lambda i: (i, 0))],
        out_shape=[jax.ShapeDtypeStruct((M, D_MODEL), BF16), jax.ShapeDtypeStruct((M, ne), F32)],
        compiler_params=_params(("arbitrary",)),
        name="moe_router",
    )(x, g, shift[0], shift[1], scale[0], scale[1], w_router, b_router)


def _moe_dense_kernel(h_ref, gates_ref, wgu_ref, bgu_ref, wdn_ref, bdn_ref, x_ref, gp_ref, gs_ref,
                      o_ref, acc_ref, *, n_p, d_ff):
    i = pl.program_id(0)
    e = pl.program_id(1)

    @pl.when(e == 0)
    def _():
        acc_ref[...] = jnp.zeros_like(acc_ref)

    gu = jnp.dot(h_ref[...], wgu_ref[0], preferred_element_type=F32) + bgu_ref[0]
    gate = jnp.minimum(gu[:, :d_ff], SWIGLU_LIMIT)
    up = jnp.clip(gu[:, d_ff:], -SWIGLU_LIMIT, SWIGLU_LIMIT)
    act = (up + 1.0) * gate * jax.nn.sigmoid(SWIGLU_ALPHA * gate)
    y = jnp.dot(_bf(act), wdn_ref[0], preferred_element_type=F32) + bdn_ref[0]
    gates = gates_ref[...]
    lane = lax.broadcasted_iota(jnp.int32, gates.shape, 1)
    g_e = jnp.sum(jnp.where(lane == e, gates, 0.0), axis=-1, keepdims=True)
    acc_ref[...] += g_e * y

    @pl.when(e == pl.num_programs(1) - 1)
    def _():
        o_ref[...] = x_ref[...] + _pick_mod(i, n_p, gp_ref, gs_ref) * acc_ref[...]


def moe_dense_call(dm, h, gates, w_gu, b_gu, w_dn, b_dn, x, gate):
    tm, M = dm.tm, dm.M
    ne, _, n2 = w_gu.shape
    d_ff = n2 // 2
    sp, ss = dm.mod_specs()
    return pl.pallas_call(
        functools.partial(_moe_dense_kernel, n_p=dm.n_p, d_ff=d_ff),
        grid=(dm.n_t, ne),
        in_specs=[pl.BlockSpec((tm, D_MODEL), lambda i, e: (i, 0)),
                  pl.BlockSpec((tm, ne), lambda i, e: (i, 0)),
                  pl.BlockSpec((1, D_MODEL, n2), lambda i, e: (e, 0, 0)),
                  pl.BlockSpec((1, 1, n2), lambda i, e: (e, 0, 0)),
                  pl.BlockSpec((1, d_ff, D_MODEL), lambda i, e: (e, 0, 0)),
                  pl.BlockSpec((1, 1, D_MODEL), lambda i, e: (e, 0, 0)),
                  pl.BlockSpec((tm, D_MODEL), lambda i, e: (i, 0)), sp, ss],
        out_specs=pl.BlockSpec((tm, D_MODEL), lambda i, e: (i, 0)),
        out_shape=jax.ShapeDtypeStruct((M, D_MODEL), F32),
        scratch_shapes=[pltpu.VMEM((tm, D_MODEL), F32)],
        compiler_params=_params(("arbitrary", "arbitrary")),
        name="moe_experts",
    )(h, gates, w_gu, b_gu.reshape(ne, 1, n2), w_dn, b_dn.reshape(ne, 1, D_MODEL), x, gate[0], gate[1])


def _final_norm_kernel(x_ref, g_ref, o_ref):
    o_ref[...] = _rms(x_ref[...], g_ref[...])


def final_norm_call(dm, x, g):
    tm = dm.tm
    return pl.pallas_call(
        _final_norm_kernel,
        grid=(dm.n_t,),
        in_specs=[pl.BlockSpec((tm, D_MODEL), lambda i: (i, 0)), pl.BlockSpec((1, D_MODEL), lambda i: (0, 0))],
        out_specs=pl.BlockSpec((tm, D_MODEL), lambda i: (i, 0)),
        out_shape=jax.ShapeDtypeStruct((dm.M, D_MODEL), F32),
        compiler_params=_params(("arbitrary",)),
        name="final_norm",
    )(x, g)


def _prep_even(w_in, b_f, w_cmp, pe_cmp, w_out):
    a_q, a_kv, a_g, b_q = H_A * HEAD_DIM, KV_W, 3 * H_A, H_B * HEAD_DIM
    o = np.cumsum([0, a_q, a_kv, a_kv, a_kv, a_g, b_q, KV_W, H_B])
    q_a, kv_cmp, kv_sel, kv_win, g_a, q_b, kv_b, f_b = [w_in[:, o[i]:o[i + 1]] for i in range(8)]
    pad = jnp.zeros((D_MODEL, 128 - H_B - a_g), w_in.dtype)
    w_perm = _bf(jnp.concatenate([q_a, q_b, kv_cmp, kv_sel, kv_win, kv_b, f_b, g_a, pad], axis=1))
    bf_row = jnp.concatenate([b_f, jnp.zeros((128 - H_B,), F32)])[None]
    eye = jnp.eye(2, dtype=F32)
    w_big = _bf(jnp.einsum('kldo,kK,gG->lkgdKGo', w_cmp, eye, eye).reshape(L_CMP * KV_W, KV_W))
    pe_flat = jnp.broadcast_to(jnp.transpose(pe_cmp, (1, 0, 2))[:, :, None, :],
                               (L_CMP, 2, G_A, HEAD_DIM)).reshape(1, L_CMP * KV_W)
    w_out_h = _bf(w_out.reshape(H_A + H_B, HEAD_DIM, D_MODEL))
    return w_perm, bf_row, w_big, pe_flat, w_out_h


def _rot_cols(w):
    half = D_ROPE // 2
    return jnp.concatenate([-w[..., half:], w[..., :half]], axis=-1)


def _prep_odd(w_in, w_qb, w_kvb, w_out):
    kpe_w = w_in[:, Q_LORA + KV_LORA:]
    pad = jnp.zeros((D_MODEL, 128 - 2 * D_ROPE), w_in.dtype)
    w_perm = _bf(jnp.concatenate([w_in[:, :Q_LORA + KV_LORA], kpe_w, _rot_cols(kpe_w), pad], axis=1))
    wq = w_qb.reshape(Q_LORA, H_C, D_NOPE + D_ROPE)
    wkv = w_kvb.reshape(KV_LORA, H_C, D_NOPE + D_V)
    nope_h = jnp.transpose(wq[..., :D_NOPE], (1, 0, 2))
    w_uk_h = jnp.transpose(wkv[..., :D_NOPE], (1, 0, 2))
    w_qabs = bmm_call(nope_h, w_uk_h, True, "w_q_absorb")
    w_qabs = _bf(jnp.transpose(w_qabs, (1, 0, 2)).reshape(Q_LORA, H_C * KV_LORA))
    pe_w = wq[..., D_NOPE:]
    w_qpe = _bf(jnp.concatenate([pe_w.reshape(Q_LORA, H_C * D_ROPE),
                                 _rot_cols(pe_w).reshape(Q_LORA, H_C * D_ROPE)], axis=1))
    w_uv_h = jnp.transpose(wkv[..., D_NOPE:], (1, 0, 2))
    w_comb = bmm_call(w_uv_h, w_out.reshape(H_C, D_V, D_MODEL), False, "w_out_absorb")
    w_comb = _bf(w_comb.reshape(H_C * KV_LORA, D_MODEL))
    return w_perm, w_qabs, w_qpe, w_comb


def _rope_tables(dm, past):
    half = D_ROPE // 2
    inv = ROPE_THETA ** (-jnp.arange(half, dtype=F32) / half)
    pos = jnp.concatenate([jnp.tile(jnp.arange(dm.T, dtype=jnp.int32), dm.B),
                           jnp.tile(past + jnp.arange(dm.DT, dtype=jnp.int32), dm.DB)])
    ang = pos.astype(F32)[:, None] * inv[None, :]
    cos32 = jnp.tile(jnp.cos(ang), (1, 2))
    sin32 = jnp.tile(jnp.sin(ang), (1, 2))
    return cos32, sin32, jnp.tile(cos32, (1, H_C)), jnp.tile(sin32, (1, H_C))


def _split3_cols(c):
    hi, mid, lo = _split3(c)
    return [hi, mid, lo]


def _fox_augment(dm, q_hm, kvh, cum):
    Mp, M = dm.Mp, dm.M
    cum = jnp.pad(cum.reshape(Mp, H_B), ((0, M - Mp), (0, 0)))
    ones = jnp.ones((M, 3), BF16)
    zeros3 = jnp.zeros((M, 3), BF16)
    qs, ks = [], []
    for g in range(G_B):
        kcols = [kvh[8 + g]]
        for r in range(R_B):
            kcols += _split3_cols(-cum[:, g * R_B + r:g * R_B + r + 1])
        kcols += [ones, jnp.zeros((M, 128 - HEAD_DIM - 3 * R_B - 3), BF16)]
        ks.append(jnp.concatenate(kcols, axis=1))
        for r in range(R_B):
            h = g * R_B + r
            qcols = [q_hm[H_A + h]] + [ones if rr == r else zeros3 for rr in range(R_B)]
            qcols += _split3_cols(cum[:, h:h + 1])
            qcols += [jnp.zeros((M, 128 - HEAD_DIM - 3 * R_B - 3), BF16)]
            qs.append(jnp.concatenate(qcols, axis=1))
    return jnp.stack(qs), jnp.stack(ks)


NEW_PAD = 8
NEW_LANES = 128


def _page_copy(pool_ref, buf_ref, sem, pt_ref, b, slot, j, rows):
    return pltpu.make_async_copy(pool_ref.at[pt_ref[b, j]], buf_ref.at[slot, pl.ds(j * rows, rows)], sem.at[slot])


def _fetch_pages(pools, pt_ref, n_pages):
    b = pl.program_id(0)
    nb = pl.num_programs(0)
    slot = b % 2

    def start(bb, sl):
        for pool_ref, buf_ref, sem, rows in pools:
            for j in range(n_pages):
                _page_copy(pool_ref, buf_ref, sem, pt_ref, bb, sl, j, rows).start()

    @pl.when(b == 0)
    def _():
        start(b, slot)

    @pl.when(b + 1 < nb)
    def _():
        start(b + 1, 1 - slot)

    for pool_ref, buf_ref, sem, rows in pools:
        for j in range(n_pages):
            _page_copy(pool_ref, buf_ref, sem, pt_ref, b, slot, j, rows).wait()
    return slot


def _softmax2(lg_p, lg_n):
    m = jnp.maximum(jnp.max(lg_p, axis=-1, keepdims=True), jnp.max(lg_n, axis=-1, keepdims=True))
    e_p = jnp.exp(lg_p - m)
    e_n = jnp.exp(lg_n - m)
    s = jnp.sum(e_p, axis=-1, keepdims=True) + jnp.sum(e_n, axis=-1, keepdims=True)
    return e_p / s, e_n / s


def _t5_decode_kernel(tabt_ref, o_ref, *, kind, past, n_past, dt, n_new):
    rows, lanes = o_ref.shape
    r = lax.broadcasted_iota(jnp.int32, (rows, lanes), 0)
    l = lax.broadcasted_iota(jnp.int32, (rows, lanes), 1)
    qpos = past + r // H_A
    u = l - n_past
    if kind == "sel":
        kpos = jnp.where(l < n_past, l, past + u)
        ok = kpos <= qpos
    elif kind == "win":
        kpos = jnp.where(l < n_past, past - n_past + l, past + u)
        ok = (kpos <= qpos) & (qpos - kpos < WINDOW) & (kpos >= 0)
    else:
        kpos = jnp.where(l < n_past, (l + 1) * L_CMP - 1, past + (u + 1) * L_CMP - 1)
        ok = kpos <= qpos
    ok = ok & (u < n_new)
    bucket = _t5_bucket(qpos - kpos)
    acc = jnp.zeros((rows, lanes), F32)
    for b in range(N_BUCKETS):
        acc = jnp.where(bucket == b, tabt_ref[:, b:b + 1], acc)
    o_ref[...] = jnp.where(ok, acc, NEG_BIG)


def t5_decode_call(t5_table, kind, past, n_past, dt, n_new):
    rows = dt * H_A
    tabt = jnp.tile(t5_table.T, (dt, 1))
    return pl.pallas_call(
        functools.partial(_t5_decode_kernel, kind=kind, past=past, n_past=n_past, dt=dt, n_new=n_new),
        out_shape=jax.ShapeDtypeStruct((rows, n_past + NEW_LANES), F32),
        compiler_params=_params(None),
        name="t5_decode_" + kind,
    )(tabt)


def _dec_cmp_kernel(pt_ref, q_ref, pool_ref, cnew_ref, bias_ref, o_ref, sel_ref, buf, sem, *,
                    n_pages, past, dt, n_new, n_sel_blocks, k_sel):
    slot = _fetch_pages([(pool_ref, buf, sem, PAGE_SIZE // L_CMP)], pt_ref, n_pages)
    ncp = buf.shape[1]
    q = q_ref[0]
    c = buf[slot]
    cn = cnew_ref[0]
    lg_p = _dot_nt(q, c[:, :128]) + bias_ref[:, :ncp]
    lg_n = _dot_nt(q, cn[:, :128]) + bias_ref[:, ncp:ncp + NEW_PAD]
    p_p, p_n = _softmax2(lg_p, lg_n)
    o_ref[0] = _dot(p_p, c[:, 128:]) + _dot(p_n, cn[:, 128:])
    n_tg = dt * G_A
    d = functools.partial(jnp.dot, preferred_element_type=F32)
    nsp = sel_ref.shape[2]

    def group_sum(p):
        return jnp.concatenate([jnp.sum(p[i * R_A:(i + 1) * R_A], axis=0, keepdims=True) for i in range(n_tg)],
                               axis=0)

    pair = (lax.broadcasted_iota(jnp.int32, (ncp, nsp), 0) * L_CMP // L_SEL
            == lax.broadcasted_iota(jnp.int32, (ncp, nsp), 1)).astype(BF16)
    hi, mid, lo = _split3(group_sum(p_p))
    imp_s = d(hi, pair) + d(mid, pair) + d(lo, pair)
    blk = lax.broadcasted_iota(jnp.int32, (n_tg, nsp), 1)
    imp_n = group_sum(p_n)
    for u in range(n_new):
        imp_s = imp_s + jnp.where(blk == (past + u * L_CMP) // L_SEL, imp_n[:, u:u + 1], 0.0)
    cur = (past + lax.broadcasted_iota(jnp.int32, (n_tg, nsp), 0) // G_A) // L_SEL
    valid = (blk <= cur) & (blk < n_sel_blocks)
    forced = (blk == 0) | (blk == cur) | (blk == cur - 1)
    score = jnp.where(forced & valid, jnp.inf, jnp.where(valid, imp_s, -jnp.inf))
    chosen = jnp.zeros((n_tg, nsp), jnp.bool_)
    for _ in range(k_sel):
        m = jnp.max(score, axis=-1, keepdims=True)
        idx = jnp.min(jnp.where(score == m, blk, nsp), axis=-1, keepdims=True)
        hit = blk == idx
        chosen = chosen | hit
        score = jnp.where(hit, -jnp.inf, score)
    sel_ref[0] = jnp.where(chosen, 0.0, SEL_NEG).astype(BF16)


def _dec_sel_kernel(pt_ref, q_ref, pool_ref, knew_ref, selneg_ref, bias_ref, o_ref, buf, sem, onehot, *,
                    n_pages, past, dt):
    b = pl.program_id(0)
    nsp, lanes = onehot.shape

    @pl.when(b == 0)
    def _():
        l = lax.broadcasted_iota(jnp.int32, (nsp, lanes), 1)
        onehot[...] = (lax.broadcasted_iota(jnp.int32, (nsp, lanes), 0) == l // L_SEL).astype(BF16)

    slot = _fetch_pages([(pool_ref, buf, sem, PAGE_SIZE)], pt_ref, n_pages)
    q = q_ref[0]
    sel_tg = selneg_ref[0].astype(F32)
    sel_rows = _bf(jnp.concatenate([jnp.broadcast_to(sel_tg[i:i + 1], (R_A, nsp)) for i in range(dt * G_A)],
                                   axis=0))
    sb = jnp.dot(sel_rows, onehot[...], preferred_element_type=F32) + bias_ref[...]
    kv = _bf(buf[slot])
    kn = _bf(knew_ref[0])
    lg_p = _dot_nt(q, kv[:, :128]) + sb[:, :past]
    lg_n = _dot_nt(q, kn[:, :128]) + sb[:, past:past + NEW_PAD]
    p_p, p_n = _softmax2(lg_p, lg_n)
    o_ref[0] = _dot(p_p, kv[:, 128:]) + _dot(p_n, kn[:, 128:])


def _dec_fox_kernel(pt_ref, q_ref, pool_ref, lf_ref, knew_ref, lfn_ref, o_ref, buf, sem, buflf, semlf, *,
                    n_pages, past, dt):
    slot = _fetch_pages([(pool_ref, buf, sem, PAGE_SIZE), (lf_ref, buflf, semlf, H_B)], pt_ref, n_pages)
    q = q_ref[0]
    rows = q.shape[0]
    x = buflf[slot]
    n = x.shape[0]
    tri = (lax.broadcasted_iota(jnp.int32, (PAGE_SIZE, PAGE_SIZE), 0)
           <= lax.broadcasted_iota(jnp.int32, (PAGE_SIZE, PAGE_SIZE), 1)).astype(BF16)
    d = functools.partial(jnp.dot, preferred_element_type=F32)

    def cum_lanes(v):
        hi, mid, lo = _split3(v)
        return d(hi, tri) + d(mid, tri) + d(lo, tri)

    w = cum_lanes(x)
    run = jnp.zeros((H_B, 1), F32)
    pieces = []
    for j in range(n_pages):
        wj = w[j * H_B:(j + 1) * H_B]
        pieces.append(wj + run)
        run = run + wj[:, PAGE_SIZE - 1:PAGE_SIZE]
    ck = jnp.concatenate(pieces, axis=1)
    cnew = cum_lanes(lfn_ref[0]) + run
    ck32 = jnp.concatenate([ck] * dt, axis=0)
    cn32 = jnp.concatenate([cnew[:, :NEW_PAD]] * dt, axis=0)
    cq32 = jnp.concatenate([cnew[:, t:t + 1] for t in range(dt)], axis=0)
    kv = _bf(buf[slot])
    kn = _bf(knew_ref[0])
    lg_p = _dot_nt(q, kv[:, :128]) + (cq32 - ck32)
    u = lax.broadcasted_iota(jnp.int32, (rows, NEW_PAD), 1)
    t = lax.broadcasted_iota(jnp.int32, (rows, NEW_PAD), 0) // H_B
    lg_n = jnp.where(u <= t, _dot_nt(q, kn[:, :128]) + (cq32 - cn32), NEG_BIG)
    p_p, p_n = _softmax2(lg_p, lg_n)
    o_ref[0] = _dot(p_p, kv[:, 128:]) + _dot(p_n, kn[:, 128:])


def _dec_win_kernel(q_ref, win_ref, knew_ref, bias_ref, o_ref):
    q = q_ref[0]
    kv = _bf(win_ref[0])
    kn = _bf(knew_ref[0])
    wb = kv.shape[0]
    lg_p = _dot_nt(q, kv[:, :128]) + bias_ref[:, :wb]
    lg_n = _dot_nt(q, kn[:, :128]) + bias_ref[:, wb:wb + NEW_PAD]
    p_p, p_n = _softmax2(lg_p, lg_n)
    o_ref[0] = _dot(p_p, kv[:, 128:]) + _dot(p_n, kn[:, 128:])


def _dec_mla_kernel(pt_ref, q_ref, qpe_ref, pool_ref, poolpe_ref, cnew_ref, penew_ref, o_ref,
                    buf, sem, bufpe, sempe, *, n_pages, layer):
    slot = _fetch_pages([(pool_ref.at[layer], buf, sem, PAGE_SIZE), (poolpe_ref.at[layer], bufpe, sempe, PAGE_SIZE)],
                        pt_ref, n_pages)
    q, qpe = q_ref[...], qpe_ref[...]
    rows = q.shape[0]
    c = _bf(buf[slot])
    cn = _bf(cnew_ref[0])
    lg_p = _dot_nt(q, c) + _dot_nt(qpe, bufpe[slot])
    u = lax.broadcasted_iota(jnp.int32, (rows, NEW_PAD), 1)
    t = lax.broadcasted_iota(jnp.int32, (rows, NEW_PAD), 0) // H_C
    lg_n = jnp.where(u <= t, _dot_nt(q, cn) + _dot_nt(qpe, penew_ref[0]), NEG_BIG)
    p_p, p_n = _softmax2(lg_p, lg_n)
    o_ref[...] = _bf(_dot(p_p, c) + _dot(p_n, cn))


def _paged_call(kern, name, dm, page_table, in_arrays, in_specs, out_shape, out_specs, scratch):
    grid_spec = pltpu.PrefetchScalarGridSpec(
        num_scalar_prefetch=1, grid=(dm.DB,), in_specs=in_specs, out_specs=out_specs, scratch_shapes=scratch)
    return pl.pallas_call(kern, grid_spec=grid_spec, out_shape=out_shape,
                          compiler_params=_params(("arbitrary",)), name=name)(page_table, *in_arrays)


_ANY = pl.BlockSpec(memory_space=pl.ANY)


def _seq_spec(*tail):
    return pl.BlockSpec((1,) + tail, lambda b, *_: (b,) + (0,) * len(tail))


def _const_spec(shape):
    return pl.BlockSpec(shape, lambda b, *_: (0,) * len(shape))


def _q_blockdiag(dm, q_heads):
    DB, DT = dm.DB, dm.DT
    q = jnp.transpose(q_heads.reshape(H_A, DB, DT, HEAD_DIM), (1, 2, 0, 3))
    g0 = (jnp.arange(H_A) < R_A)[None, None, :, None]
    z = jnp.zeros_like(q)
    return jnp.concatenate([jnp.where(g0, q, z), jnp.where(g0, z, q)], axis=-1).reshape(DB, DT * H_A, 2 * HEAD_DIM)


def _from_blockdiag(dm, o):
    DB, DT = dm.DB, dm.DT
    o = o.reshape(DB, DT, H_A, 2, HEAD_DIM)
    g0 = (jnp.arange(H_A) < R_A)[None, None, :, None]
    o = jnp.where(g0, o[..., 0, :], o[..., 1, :])
    return jnp.transpose(o, (2, 0, 1, 3)).reshape(H_A, DB * DT, HEAD_DIM)


def _new_rows(dm, a, width):
    return jnp.pad(a.reshape(dm.DB, dm.DT, width), ((0, 0), (0, NEW_PAD - dm.DT), (0, 0)))


def decode_even(dm, page_table, q_hm, kv_cmp, kv_sel, kv_win, kv_b, logf, pool_cmp, pool_sel, pool_fox_kv,
                pool_fox_logf, win_buf, pe_flat, w_big, dec_bias):
    DB, DT, Mp = dm.DB, dm.DT, dm.Mp
    P = pool_cmp.shape[0]
    n_pages = page_table.shape[1]
    past = n_pages * PAGE_SIZE
    total = past + DT
    padded = -(-total // L_SEL) * L_SEL
    ns = padded // L_SEL
    nsp = -(-ns // 128) * 128
    rows = DT * H_A
    bpp = PAGE_SIZE // L_CMP
    qa = _q_blockdiag(dm, q_hm[:H_A, Mp:])
    qb = _q_blockdiag(dm, q_hm[H_A:, Mp:])
    o_shape = jax.ShapeDtypeStruct((DB, rows, 2 * HEAD_DIM), F32)
    q_spec = _seq_spec(rows, 2 * HEAD_DIM)
    new_spec = _seq_spec(NEW_PAD, KV_W)
    bias_cmp, bias_sel, bias_win = dec_bias

    pool_c, _ = compress_call(pool_cmp.reshape(P * PAGE_SIZE, KV_W), pe_flat, w_big)
    n_new = (padded - past) // L_CMP
    new_rows = jnp.pad(kv_cmp[Mp:].reshape(DB, DT * KV_W), ((0, 0), (0, (padded - total) * KV_W)))
    c_new, _ = compress_call(new_rows.reshape(DB * n_new * L_CMP, KV_W), pe_flat, w_big)
    c_new = jnp.pad(c_new.reshape(DB, n_new, KV_W), ((0, 0), (0, NEW_PAD - n_new), (0, 0)))
    o_cmp, selneg = _paged_call(
        functools.partial(_dec_cmp_kernel, n_pages=n_pages, past=past, dt=DT, n_new=n_new, n_sel_blocks=ns,
                          k_sel=min(N_SEL, ns)),
        "decode_cmp", dm, page_table,
        [qa, pool_c.reshape(P, bpp, KV_W), c_new, bias_cmp],
        [q_spec, _ANY, new_spec, _const_spec(bias_cmp.shape)],
        [o_shape, jax.ShapeDtypeStruct((DB, DT * G_A, nsp), BF16)],
        [q_spec, _seq_spec(DT * G_A, nsp)],
        [pltpu.VMEM((2, n_pages * bpp, KV_W), F32), pltpu.SemaphoreType.DMA((2,))])

    o_sel = _paged_call(
        functools.partial(_dec_sel_kernel, n_pages=n_pages, past=past, dt=DT),
        "decode_sel", dm, page_table,
        [qa, pool_sel.reshape(P, PAGE_SIZE, KV_W), _new_rows(dm, kv_sel[Mp:], KV_W), selneg, bias_sel],
        [q_spec, _ANY, new_spec, _seq_spec(DT * G_A, nsp), _const_spec(bias_sel.shape)],
        o_shape, q_spec,
        [pltpu.VMEM((2, past, KV_W), F32), pltpu.SemaphoreType.DMA((2,)),
         pltpu.VMEM((nsp, past + NEW_LANES), BF16)])

    lf_pool_t = jnp.transpose(pool_fox_logf, (0, 2, 1))
    lf_new_t = jnp.pad(jnp.transpose(logf[Mp:].reshape(DB, DT, H_B), (0, 2, 1)),
                       ((0, 0), (0, 0), (0, PAGE_SIZE - DT)))
    o_fox = _paged_call(
        functools.partial(_dec_fox_kernel, n_pages=n_pages, past=past, dt=DT),
        "decode_fox", dm, page_table,
        [qb, pool_fox_kv.reshape(P, PAGE_SIZE, KV_W), lf_pool_t, _new_rows(dm, kv_b[Mp:], KV_W), lf_new_t],
        [q_spec, _ANY, _ANY, new_spec, _seq_spec(H_B, PAGE_SIZE)],
        o_shape, q_spec,
        [pltpu.VMEM((2, past, KV_W), F32), pltpu.SemaphoreType.DMA((2,)),
         pltpu.VMEM((2, n_pages * H_B, PAGE_SIZE), F32), pltpu.SemaphoreType.DMA((2,))])

    wb = win_buf.shape[1]
    o_win = pl.pallas_call(
        _dec_win_kernel,
        grid=(DB,),
        in_specs=[q_spec, _seq_spec(wb, KV_W), new_spec, _const_spec(bias_win.shape)],
        out_specs=q_spec,
        out_shape=o_shape,
        compiler_params=_params(("arbitrary",)),
        name="decode_win",
    )(qa, win_buf.reshape(DB, wb, KV_W), _new_rows(dm, kv_win[Mp:], KV_W), bias_win)
    return tuple(_from_blockdiag(dm, o) for o in (o_cmp, o_sel, o_win, o_fox))


def decode_odd(dm, page_table, q_lat, q_pe, ckv, kpe_b, pool_ckv, pool_kpe, layer):
    DB, DT, Mp = dm.DB, dm.DT, dm.Mp
    n_pages = page_table.shape[1]
    past = n_pages * PAGE_SIZE
    rows = DT * H_C
    row0 = Mp * H_C // rows
    q_spec = lambda w: pl.BlockSpec((rows, w), lambda b, *_: (row0 + b, 0))
    return _paged_call(
        functools.partial(_dec_mla_kernel, n_pages=n_pages, layer=layer),
        "decode_mla", dm, page_table,
        [q_lat, q_pe, pool_ckv, pool_kpe, _new_rows(dm, ckv[Mp:], KV_LORA), _new_rows(dm, kpe_b[Mp:], D_ROPE)],
        [q_spec(KV_LORA), q_spec(D_ROPE), _ANY, _ANY, _seq_spec(NEW_PAD, KV_LORA), _seq_spec(NEW_PAD, D_ROPE)],
        jax.ShapeDtypeStruct((DB * rows, KV_LORA), BF16),
        pl.BlockSpec((rows, KV_LORA), lambda b, *_: (b, 0)),
        [pltpu.VMEM((2, past, KV_LORA), F32), pltpu.SemaphoreType.DMA((2,)),
         pltpu.VMEM((2, past, D_ROPE), F32), pltpu.SemaphoreType.DMA((2,))])


def kernel(x_prompt, x_sample, c_prompt, c_sample, cache_nsa_cmp, cache_nsa_sel, cache_fox_kv, cache_fox_logf, state_nsa_win, cache_mla_ckv, cache_mla_kpe, page_table, t5_table, w_in_ab, b_forget, w_cmp, pe_cmp, w_out_ab, w_in_c, g_qnorm, w_qb, g_kvnorm, w_kvb, w_out_c, g_norm_mix, g_norm_ffn, w_ada, b_ada, w_router, b_router, w_moe_gu, b_moe_gu, w_moe_down, b_moe_down, g_final):
    B, T, _ = x_prompt.shape
    DB, DT, _ = x_sample.shape
    dm = Dims(B, T, DB, DT)
    Mp, M = dm.Mp, dm.M
    depth = w_ada.shape[0]
    past = page_table.shape[1] * PAGE_SIZE
    kv5 = (2, G_A, HEAD_DIM)

    x = jnp.concatenate([x_prompt.reshape(Mp, D_MODEL), x_sample.reshape(dm.Ms, D_MODEL)], axis=0)
    mod = ada_call(jnp.concatenate([c_prompt, c_sample], axis=0), w_ada, b_ada)

    def mod_vec(layer, k):
        m = mod[layer, :, k * D_MODEL:(k + 1) * D_MODEL]
        return m[:B].reshape(B, 1, D_MODEL), jnp.repeat(m[B:], DT, axis=0)

    strips = t5_strips_call(t5_table)
    cos32, sin32, cos_h, sin_h = _rope_tables(dm, past)
    n_pages = page_table.shape[1]
    wb = state_nsa_win.shape[2]
    n_new_cmp = (-(-(past + DT) // L_SEL) * L_SEL - past) // L_CMP
    dec_bias = (t5_decode_call(t5_table, "cmp", past, n_pages * (PAGE_SIZE // L_CMP), DT, n_new_cmp),
                t5_decode_call(t5_table, "sel", past, past, DT, DT),
                t5_decode_call(t5_table, "win", past, wb, DT, DT))

    even_p, even_s, odd_p, odd_s = [], [], [], []
    for layer in range(depth):
        shift0, scale0, gate0, shift1, scale1, gate1 = [mod_vec(layer, k) for k in range(6)]
        g_mix = g_norm_mix[layer][None]
        if layer % 2 == 0:
            e = layer // 2
            w_perm, bf_row, w_big, pe_flat, w_out_h = _prep_even(w_in_ab[e], b_forget[e], w_cmp[e], pe_cmp[e],
                                                                 w_out_ab[e])
            kv_cmp, kv_sel, kv_win, kv_b, gf, logf, q_hm, kvh = proj_even_call(dm, x, g_mix, shift0, scale0,
                                                                              w_perm, bf_row)
            _, cmp_hm = compress_call(kv_cmp[:Mp], pe_flat, w_big)
            o_cmp, selneg = cmp_select_prompt_call(dm, t5_table, q_hm, cmp_hm)
            o_sel = flash_gqa_call(dm, "sel", q_hm, 0, kvh, 0, kvh, 2, H_A, selneg=selneg, strips=strips)
            o_win = flash_gqa_call(dm, "win", q_hm, 0, kvh, 4, kvh, 6, H_A, strips=strips)
            logf_t = jnp.transpose(logf[:Mp].reshape(B, T, H_B), (0, 2, 1)).reshape(B * H_B, T)
            cum = jnp.transpose(cumsum_call(logf_t).reshape(B, H_B, T), (0, 2, 1))
            q_fox, k_fox = _fox_augment(dm, q_hm, kvh, cum)
            o_fox = flash_gqa_call(dm, "fox", q_fox, 0, k_fox, 0, kvh, 10, H_B)
            d_cmp, d_sel, d_win, d_fox = decode_even(
                dm, page_table, q_hm, kv_cmp, kv_sel, kv_win, kv_b, logf, cache_nsa_cmp[e], cache_nsa_sel[e],
                cache_fox_kv[e], cache_fox_logf[e], state_nsa_win[e], pe_flat, w_big, dec_bias)
            win_new = jnp.concatenate([state_nsa_win[e], kv_win[Mp:].reshape((DB, DT) + kv5)], axis=1)[:, DT:]
            x = out_even_call(dm, x, gate0, gf, (o_cmp, o_sel, o_win, o_fox), (d_cmp, d_sel, d_win, d_fox), w_out_h)
            n_keep = min(WINDOW, T)
            sp = lambda a: a[:Mp].reshape((B, T) + kv5)
            ss = lambda a: a[Mp:].reshape((DB, DT) + kv5)
            even_p.append((sp(kv_cmp), sp(kv_sel), sp(kv_b), logf[:Mp].reshape(B, T, H_B),
                           sp(kv_win)[:, T - n_keep:]))
            even_s.append((ss(kv_cmp), ss(kv_sel), ss(kv_b), logf[Mp:].reshape(DB, DT, H_B), win_new))
        else:
            o = layer // 2
            w_perm, w_qabs, w_qpe, w_comb = _prep_odd(w_in_c[o], w_qb[o], w_kvb[o], w_out_c[o])
            cqn, ckv, ckv_b, kpe, kpe_b = proj_odd_call(dm, x, g_mix, shift0, scale0, w_perm,
                                                        g_qnorm[o][None], g_kvnorm[o][None], cos32, sin32)
            q_lat, q_pe = q_mla_call(dm, cqn, w_qabs, w_qpe, cos_h, sin_h)
            q_lat = q_lat.reshape(M * H_C, KV_LORA)
            q_pe = q_pe.reshape(M * H_C, D_ROPE)
            o_lat = flash_mla_call(dm, q_lat, q_pe, ckv_b, kpe_b)
            d_lat = decode_odd(dm, page_table, q_lat, q_pe, ckv, kpe_b, cache_mla_ckv, cache_mla_kpe, o)
            x = out_odd_call(dm, x, gate0, o_lat.reshape(Mp, H_C * KV_LORA),
                             d_lat.reshape(dm.Ms, H_C * KV_LORA), w_comb)
            odd_p.append((ckv[:Mp].reshape(B, T, KV_LORA), kpe[:Mp].reshape(B, T, D_ROPE)))
            odd_s.append((ckv[Mp:].reshape(DB, DT, KV_LORA), kpe[Mp:].reshape(DB, DT, D_ROPE)))
        h, gates = moe_pre_call(dm, x, g_norm_ffn[layer][None], shift1, scale1, w_router[layer],
                                b_router[layer][None])
        x = moe_dense_call(dm, h, gates, _bf(w_moe_gu[layer]), b_moe_gu[layer], _bf(w_moe_down[layer]),
                           b_moe_down[layer], x, gate1)

    y = final_norm_call(dm, x, g_final[None])
    outs = [y[:Mp].reshape(B, T, D_MODEL), y[Mp:].reshape(DB, DT, D_MODEL)]
    ep = [jnp.stack(a) for a in zip(*even_p)]
    es = [jnp.stack(a) for a in zip(*even_s)]
    op = [jnp.stack(a) for a in zip(*odd_p)]
    os_ = [jnp.stack(a) for a in zip(*odd_s)]
    for a, b in zip(ep, es):
        outs += [a, b]
    for a, b in zip(op, os_):
        outs += [a, b]
    return tuple(outs)
```

```python
import functools
import math

import numpy as np
import jax
import jax.numpy as jnp
from jax import lax
from jax.experimental import pallas as pl
from jax.experimental.pallas import tpu as pltpu

F32 = jnp.float32
BF16 = jnp.bfloat16

D_MODEL = 1024
HEAD_DIM = 64
PAGE_SIZE = 128
H_A, G_A = 8, 2
R_A = H_A // G_A
L_CMP, L_SEL, N_SEL, WINDOW = 32, 64, 16, 512
H_B, G_B = 8, 2
R_B = H_B // G_B
H_C, Q_LORA, KV_LORA, D_NOPE, D_ROPE, D_V = 16, 512, 256, 64, 32, 64
ROPE_THETA = 10000.0
MLA_SCALE = (D_NOPE + D_ROPE) ** -0.5
N_BUCKETS, MAX_DISTANCE = 32, 128
TOP_K = 4
SWIGLU_LIMIT, SWIGLU_ALPHA = 7.0, 1.702
RMS_EPS = 1e-6
KV_W = 2 * G_A * HEAD_DIM

NEG_BIG = -1e30
SEL_NEG = -30000.0
TQ = 128
TK = 512
VMEM_LIMIT = 56 * 1024 * 1024


def _bf(x):
    return x.astype(BF16)


def _dot(a, b):
    return jnp.dot(_bf(a), _bf(b), preferred_element_type=F32)


def _dot_nt(a, b):
    return lax.dot_general(_bf(a), _bf(b), (((1,), (1,)), ((), ())), preferred_element_type=F32)


def _split2(x):
    hi = _bf(x)
    return hi, _bf(x - hi.astype(F32))


def _split3(x):
    hi = _bf(x)
    r = x - hi.astype(F32)
    mid = _bf(r)
    return hi, mid, _bf(r - mid.astype(F32))


def _dot3(a, b):
    ah, al = _split2(a)
    bh, bl = _split2(b)
    d = functools.partial(jnp.dot, preferred_element_type=F32)
    return d(ah, bh) + d(ah, bl) + d(al, bh)


def _dot3_nt(a, b):
    ah, al = _split2(a)
    bh, bl = _split2(b)
    d = functools.partial(lax.dot_general, dimension_numbers=(((1,), (1,)), ((), ())),
                          preferred_element_type=F32)
    return d(ah, bh) + d(ah, bl) + d(al, bh)


def _rms(x, g):
    return x * lax.rsqrt(jnp.mean(x * x, axis=-1, keepdims=True) + RMS_EPS) * g


def _log_sigmoid(z):
    return jnp.minimum(z, 0.0) - jnp.log1p(jnp.exp(-jnp.abs(z)))


def _t5_bucket(dist):
    n = jnp.maximum(dist, 0)
    max_exact = N_BUCKETS // 2
    nf = jnp.maximum(n, 1).astype(F32)
    large = max_exact + (jnp.log(nf / max_exact) / math.log(MAX_DISTANCE / max_exact)
                         * (N_BUCKETS - max_exact)).astype(jnp.int32)
    return jnp.where(n < max_exact, n, jnp.minimum(large, N_BUCKETS - 1))


def _params(sem, vmem=VMEM_LIMIT):
    return pltpu.CompilerParams(dimension_semantics=sem, vmem_limit_bytes=vmem)


class Dims:
    def __init__(self, B, T, DB, DT):
        self.B, self.T, self.DB, self.DT = B, T, DB, DT
        self.Mp, self.Ms = B * T, DB * DT
        self.M = self.Mp + self.Ms
        self.tm = math.gcd(512, self.Ms)
        assert T % self.tm == 0 and T % TK == 0 and self.tm % 8 == 0
        self.n_p = self.Mp // self.tm
        self.n_t = self.M // self.tm
        self.tiles_per_b = T // self.tm

    def mod_specs(self):
        tpb, nb, n_p, tm = self.tiles_per_b, self.B, self.n_p, self.tm
        sp = pl.BlockSpec((1, 1, D_MODEL), lambda i, *_: (jnp.minimum(i // tpb, nb - 1), 0, 0))
        ss = pl.BlockSpec((tm, D_MODEL), lambda i, *_: (jnp.maximum(i - n_p, 0), 0))
        return sp, ss


def _pick_mod(i, n_p, mp_ref, ms_ref):
    return jnp.where(i < n_p, mp_ref[0], ms_ref[...])


def _ada_kernel(c_ref, w_ref, b_ref, o_ref):
    c = c_ref[...]
    o_ref[0] = _dot3(c * jax.nn.sigmoid(c), w_ref[0]) + b_ref[0]


def ada_call(c_all, w_ada, b_ada):
    depth, _, n6 = w_ada.shape
    n_c = c_all.shape[0]
    tn = 1536
    return pl.pallas_call(
        _ada_kernel,
        grid=(depth, n6 // tn),
        in_specs=[pl.BlockSpec((n_c, D_MODEL), lambda l, j: (0, 0)),
                  pl.BlockSpec((1, D_MODEL, tn), lambda l, j: (l, 0, j)),
                  pl.BlockSpec((1, 1, tn), lambda l, j: (l, 0, j))],
        out_specs=pl.BlockSpec((1, n_c, tn), lambda l, j: (l, 0, j)),
        out_shape=jax.ShapeDtypeStruct((depth, n_c, n6), F32),
        compiler_params=_params(("arbitrary", "arbitrary")),
        name="ada_modulation",
    )(c_all, w_ada, b_ada.reshape(depth, 1, n6))


def _t5_strip_kernel(tab_ref, o_ref):
    p = pl.program_id(0)
    i = lax.broadcasted_iota(jnp.int32, (TQ, TK), 0)
    j = lax.broadcasted_iota(jnp.int32, (TQ, TK), 1)
    off = jnp.where(p < 4, p * TQ, jnp.where(p == 4, TK, 4 * TK))
    bucket = _t5_bucket(off + i - j)
    for h in range(H_A):
        acc = jnp.zeros((TQ, TK), F32)
        for b in range(N_BUCKETS):
            acc = jnp.where(bucket == b, tab_ref[b, h], acc)
        o_ref[h, 0] = acc


def t5_strips_call(t5_table):
    return pl.pallas_call(
        _t5_strip_kernel,
        grid=(6,),
        in_specs=[pl.BlockSpec(memory_space=pltpu.SMEM)],
        out_specs=pl.BlockSpec((H_A, 1, TQ, TK), lambda p: (0, p, 0, 0)),
        out_shape=jax.ShapeDtypeStruct((H_A, 6, TQ, TK), F32),
        compiler_params=_params(("arbitrary",)),
        name="t5_strips",
    )(t5_table)


def _proj_even_kernel(x_ref, g_ref, shp_ref, shs_ref, scp_ref, scs_ref, w_ref, bf_ref,
                      cmp_ref, sel_ref, win_ref, kvb_ref, gf_ref, logf_ref, q_ref, kvh_ref, *, n_p):
    i = pl.program_id(0)
    shift = _pick_mod(i, n_p, shp_ref, shs_ref)
    scale = _pick_mod(i, n_p, scp_ref, scs_ref)
    hb = _bf(_rms(x_ref[...], g_ref[...]) * (1.0 + scale) + shift)
    q = jnp.dot(hb, w_ref[:, 0:1024], preferred_element_type=F32) * (HEAD_DIM ** -0.5)
    for h in range(H_A + H_B):
        q_ref[h] = _bf(q[:, h * HEAD_DIM:(h + 1) * HEAD_DIM])
    kv = jnp.dot(hb, w_ref[:, 1024:2048], preferred_element_type=F32)
    cmp_ref[...] = kv[:, 0:256]
    sel_ref[...] = kv[:, 256:512]
    win_ref[...] = kv[:, 512:768]
    kvb_ref[...] = kv[:, 768:1024]
    for c in range(12):
        kvh_ref[c] = _bf(kv[:, 256 + c * HEAD_DIM:256 + (c + 1) * HEAD_DIM])
    gfr = jnp.dot(hb, w_ref[:, 2048:2176], preferred_element_type=F32)
    lane = lax.broadcasted_iota(jnp.int32, gfr.shape, 1)
    lsg = _log_sigmoid(gfr + bf_ref[...])
    gf_ref[...] = jnp.where(lane < H_B, lsg, jax.nn.sigmoid(gfr))
    logf_ref[...] = lsg[:, 0:H_B]


def proj_even_call(dm, x, g, shift, scale, w_perm, bf_row):
    tm, M = dm.tm, dm.M
    sp, ss = dm.mod_specs()
    row = lambda w: pl.BlockSpec((tm, w), lambda i: (i, 0))
    outs = [jax.ShapeDtypeStruct((M, KV_W), F32)] * 4 + [
        jax.ShapeDtypeStruct((M, 128), F32), jax.ShapeDtypeStruct((M, H_B), F32),
        jax.ShapeDtypeStruct((H_A + H_B, M, HEAD_DIM), BF16),
        jax.ShapeDtypeStruct((12, M, HEAD_DIM), BF16)]
    return pl.pallas_call(
        functools.partial(_proj_even_kernel, n_p=dm.n_p),
        grid=(dm.n_t,),
        in_specs=[row(D_MODEL), pl.BlockSpec((1, D_MODEL), lambda i: (0, 0)), sp, ss, sp, ss,
                  pl.BlockSpec((D_MODEL, 2176), lambda i: (0, 0)),
                  pl.BlockSpec((1, 128), lambda i: (0, 0))],
        out_specs=[row(KV_W)] * 4 + [row(128), row(H_B),
                   pl.BlockSpec((H_A + H_B, tm, HEAD_DIM), lambda i: (0, i, 0)),
                   pl.BlockSpec((12, tm, HEAD_DIM), lambda i: (0, i, 0))],
        out_shape=outs,
        compiler_params=_params(("arbitrary",)),
        name="proj_even",
    )(x, g, shift[0], shift[1], scale[0], scale[1], w_perm, bf_row)


def _compress_kernel(xk_ref, xv_ref, pe_ref, w_ref, o_ref, oh_ref):
    tb = o_ref.shape[0]
    hw = KV_W // 2
    yk = jnp.zeros((tb, hw), F32)
    yv = jnp.zeros((tb, hw), F32)
    for l in range(L_CMP):
        yk = yk + _dot(xk_ref[pl.ds(l, tb, stride=L_CMP), :] + pe_ref[l:l + 1, :hw], w_ref[l, :hw, :hw])
        yv = yv + _dot(xv_ref[pl.ds(l, tb, stride=L_CMP), :] + pe_ref[l:l + 1, hw:], w_ref[l, hw:, hw:])
    y = jnp.concatenate([yk, yv], axis=1)
    o_ref[...] = y
    for c in range(4):
        oh_ref[c] = _bf(y[:, c * HEAD_DIM:(c + 1) * HEAD_DIM])


def compress_call(rows, pe_flat, w_big):
    n_blk = rows.shape[0] // L_CMP
    tb = math.gcd(n_blk, 256)
    return pl.pallas_call(
        _compress_kernel,
        grid=(n_blk // tb,),
        in_specs=[pl.BlockSpec((tb * L_CMP, KV_W // 2), lambda i: (i, 0)),
                  pl.BlockSpec((tb * L_CMP, KV_W // 2), lambda i: (i, 1)),
                  pl.BlockSpec((L_CMP, KV_W), lambda i: (0, 0)),
                  pl.BlockSpec((L_CMP, KV_W, KV_W), lambda i: (0, 0, 0))],
        out_specs=[pl.BlockSpec((tb, KV_W), lambda i: (i, 0)),
                   pl.BlockSpec((4, tb, HEAD_DIM), lambda i: (0, i, 0))],
        out_shape=[jax.ShapeDtypeStruct((n_blk, KV_W), F32),
                   jax.ShapeDtypeStruct((4, n_blk, HEAD_DIM), BF16)],
        compiler_params=_params(("arbitrary",)),
        name="compress",
    )(rows, rows, pe_flat.reshape(L_CMP, KV_W), w_big.reshape(L_CMP, KV_W, KV_W))


def _cumsum_kernel(x_ref, o_ref, carry_ref):
    j = pl.program_id(0)

    @pl.when(j == 0)
    def _():
        carry_ref[...] = jnp.zeros_like(carry_ref)

    n = x_ref.shape[1]
    u = lax.broadcasted_iota(jnp.int32, (n, n), 0)
    s = lax.broadcasted_iota(jnp.int32, (n, n), 1)
    tri = (u <= s).astype(BF16)
    hi, mid, lo = _split3(x_ref[...])
    d = functools.partial(jnp.dot, preferred_element_type=F32)
    c = d(hi, tri) + d(mid, tri) + d(lo, tri) + carry_ref[...]
    o_ref[...] = c
    carry_ref[...] = c[:, n - 1:n]


def cumsum_call(x):
    rows, n = x.shape
    tn = math.gcd(n, 512)
    return pl.pallas_call(
        _cumsum_kernel,
        grid=(n // tn,),
        in_specs=[pl.BlockSpec((rows, tn), lambda j: (0, j))],
        out_specs=pl.BlockSpec((rows, tn), lambda j: (0, j)),
        out_shape=jax.ShapeDtypeStruct((rows, n), F32),
        scratch_shapes=[pltpu.VMEM((rows, 1), F32)],
        compiler_params=_params(("arbitrary",)),
        name="cumsum_logf",
    )(x)


def _cmp_select_kernel(tab_ref, q_ref, ck_ref, cv_ref, o_ref, sel_ref, *, n_sel_blocks, k_sel, pos0, pos_mod):
    g = pl.program_id(0) % G_A
    qi = pl.program_id(1)
    tq = q_ref.shape[1]
    nc = ck_ref.shape[1]
    row = lax.broadcasted_iota(jnp.int32, (tq, nc), 0)
    if pos_mod is None:
        tpos = pos0 + qi * tq + row
    else:
        tpos = pos0 + row % pos_mod
    c_end = (lax.broadcasted_iota(jnp.int32, (tq, nc), 1) + 1) * L_CMP - 1
    vis = c_end <= tpos
    bucket = _t5_bucket(tpos - c_end)
    q = q_ref[...]
    lg = _dot_nt(q.reshape(R_A * tq, HEAD_DIM), ck_ref[0]).reshape(R_A, tq, nc)
    imp = jnp.zeros((tq, nc), F32)
    for r in range(R_A):
        bias = jnp.zeros((tq, nc), F32)
        for b in range(N_BUCKETS):
            bias = jnp.where(bucket == b, tab_ref[b, g * R_A + r], bias)
        lr = jnp.where(vis, lg[r] + bias, NEG_BIG)
        m = jnp.max(lr, axis=-1, keepdims=True)
        e = jnp.where(vis, jnp.exp(lr - m), 0.0)
        s = jnp.sum(e, axis=-1, keepdims=True)
        p = e / jnp.where(s > 0, s, 1.0)
        o_ref[r] = _dot(p, cv_ref[0])
        imp = imp + p
    nsp = sel_ref.shape[2]
    cpb = L_SEL // L_CMP
    pr = lax.broadcasted_iota(jnp.int32, (nc, nsp), 0) // cpb
    pc = lax.broadcasted_iota(jnp.int32, (nc, nsp), 1)
    pair = (pr == pc).astype(BF16)
    hi, mid, lo = _split3(imp)
    d = functools.partial(jnp.dot, preferred_element_type=F32)
    imp_s = d(hi, pair) + d(mid, pair) + d(lo, pair)
    blk = lax.broadcasted_iota(jnp.int32, (tq, nsp), 1)
    cur = tpos[:, 0:1] // L_SEL
    valid = (blk <= cur) & (blk < n_sel_blocks)
    forced = (blk == 0) | (blk == cur) | (blk == cur - 1)
    score = jnp.where(forced & valid, jnp.inf, jnp.where(valid, imp_s, -jnp.inf))
    chosen = jnp.zeros((tq, nsp), jnp.bool_)
    for _ in range(k_sel):
        m = jnp.max(score, axis=-1, keepdims=True)
        idx = jnp.min(jnp.where(score == m, blk, nsp), axis=-1, keepdims=True)
        hit = blk == idx
        chosen = chosen | hit
        score = jnp.where(hit, -jnp.inf, score)
    sel_ref[0] = jnp.where(chosen, 0.0, SEL_NEG).astype(BF16)


def cmp_select_prompt_call(dm, t5_table, q_hm, cmp_hm):
    B, T, M = dm.B, dm.T, dm.M
    nq, nc = T // TQ, T // L_CMP
    ns = T // L_SEL
    nsp = max(128, ns)
    kern = functools.partial(_cmp_select_kernel, n_sel_blocks=ns, k_sel=min(N_SEL, ns), pos0=0, pos_mod=None)
    return pl.pallas_call(
        kern,
        grid=(B * G_A, nq),
        in_specs=[pl.BlockSpec(memory_space=pltpu.SMEM),
                  pl.BlockSpec((R_A, TQ, HEAD_DIM), lambda bg, qi: (bg % G_A, (bg // G_A) * nq + qi, 0)),
                  pl.BlockSpec((1, nc, HEAD_DIM), lambda bg, qi: (bg % G_A, bg // G_A, 0)),
                  pl.BlockSpec((1, nc, HEAD_DIM), lambda bg, qi: (G_A + bg % G_A, bg // G_A, 0))],
        out_specs=[pl.BlockSpec((R_A, TQ, HEAD_DIM), lambda bg, qi: (bg % G_A, (bg // G_A) * nq + qi, 0)),
                   pl.BlockSpec((1, TQ, nsp), lambda bg, qi: (bg % G_A, (bg // G_A) * nq + qi, 0))],
        out_shape=[jax.ShapeDtypeStruct((H_A, dm.Mp, HEAD_DIM), F32),
                   jax.ShapeDtypeStruct((G_A, dm.Mp, nsp), BF16)],
        compiler_params=_params(("arbitrary", "arbitrary")),
        name="nsa_cmp_select_prompt",
    )(t5_table, q_hm, cmp_hm, cmp_hm)


def _steps(T, window):
    qt, kt, ft = [], [], []
    for qi in range(T // TQ):
        kd = (qi * TQ) // TK
        p = (qi * TQ % TK) // TQ
        k0 = max(kd - 1, 0) if window else 0
        for ki in range(k0, kd + 1):
            strip = p if ki == kd else (4 if (ki == kd - 1 and p == 0) else 5)
            fl = (1 if ki == k0 else 0) | (2 if ki == kd else 0) | (4 if ki == kd else 0) | (strip << 4)
            qt.append(qi), kt.append(ki), ft.append(fl)
    return (np.asarray(qt, np.int32), np.asarray(kt, np.int32), np.asarray(ft, np.int32))


def _flash_kernel(qt_ref, kt_ref, ft_ref, *refs, mode, rows_per_tok):
    if mode == "mla":
        q_ref, qpe_ref, k_ref, kpe_ref, o_ref, m_s, l_s, acc_s = refs
    elif mode == "sel":
        q_ref, k_ref, v_ref, selneg_ref, strip_ref, o_ref, m_s, l_s, acc_s = refs
    elif mode == "win":
        q_ref, k_ref, v_ref, strip_ref, o_ref, m_s, l_s, acc_s = refs
    else:
        q_ref, k_ref, v_ref, o_ref, m_s, l_s, acc_s = refs
    s = pl.program_id(1)
    qi, ki, fl = qt_ref[s], kt_ref[s], ft_ref[s]
    rows = m_s.shape[0]
    tk = k_ref.shape[-2]

    @pl.when((fl & 1) != 0)
    def _init():
        m_s[...] = jnp.full_like(m_s, NEG_BIG)
        l_s[...] = jnp.zeros_like(l_s)
        acc_s[...] = jnp.zeros_like(acc_s)

    def step(masked):
        if mode == "mla":
            kb = k_ref[...]
            lg = _dot_nt(q_ref[...], kb) + _dot_nt(qpe_ref[...], kpe_ref[...])
            vb = kb
        else:
            q = q_ref[...]
            lg = _dot_nt(q.reshape(rows, q.shape[-1]), k_ref[0])
            vb = v_ref[0]
        if mode == "sel":
            nsp = selneg_ref.shape[2]
            blk = lax.broadcasted_iota(jnp.int32, (nsp, tk), 0)
            key = ki * tk + lax.broadcasted_iota(jnp.int32, (nsp, tk), 1)
            onehot = (blk == key // L_SEL).astype(BF16)
            sb = jnp.dot(selneg_ref[0], onehot, preferred_element_type=F32)
            lg = (lg.reshape(R_A, TQ, tk) + sb[None]).reshape(rows, tk)
        if mode in ("sel", "win"):
            lg = lg + strip_ref[:, 0].reshape(rows, tk)
        if masked:
            r = lax.broadcasted_iota(jnp.int32, (rows, tk), 0)
            tok = r // rows_per_tok if rows_per_tok > 1 else r % TQ
            dist = (qi * TQ + tok) - (ki * tk + lax.broadcasted_iota(jnp.int32, (rows, tk), 1))
            ok = dist >= 0
            if mode == "win":
                ok = ok & (dist < WINDOW)
            lg = jnp.where(ok, lg, NEG_BIG)
        m_prev = m_s[...]
        m_new = jnp.maximum(m_prev, jnp.max(lg, axis=-1, keepdims=True))
        alpha = jnp.exp(m_prev - m_new)
        p = jnp.exp(lg - m_new)
        l_s[...] = alpha * l_s[...] + jnp.sum(p, axis=-1, keepdims=True)
        acc_s[...] = alpha * acc_s[...] + _dot(p, vb)
        m_s[...] = m_new

    if mode == "win":
        step(True)
    else:
        @pl.when((fl & 4) != 0)
        def _diag():
            step(True)

        @pl.when((fl & 4) == 0)
        def _inner():
            step(False)

    @pl.when((fl & 2) != 0)
    def _fin():
        o = acc_s[...] / l_s[...]
        o_ref[...] = o.reshape(o_ref.shape).astype(o_ref.dtype)


def _flash_scratch(rows, dv):
    return [pltpu.VMEM((rows, 1), F32), pltpu.VMEM((rows, 1), F32), pltpu.VMEM((rows, dv), F32)]


def flash_gqa_call(dm, mode, q, q_head0, k, k_idx0, v, v_idx0, n_heads_out, selneg=None, strips=None):
    B, T, M = dm.B, dm.T, dm.M
    R, G = R_A, G_A
    nq, nk = T // TQ, T // TK
    qt, kt, ft = _steps(T, window=(mode == "win"))
    dq, dk, dv = q.shape[-1], k.shape[-1], v.shape[-1]
    qh0 = q_head0 // R
    in_specs = [
        pl.BlockSpec((R, TQ, dq), lambda bg, s, qt, kt, ft: (qh0 + bg % G, (bg // G) * nq + qt[s], 0)),
        pl.BlockSpec((1, TK, dk), lambda bg, s, qt, kt, ft: (k_idx0 + bg % G, (bg // G) * nk + kt[s], 0)),
        pl.BlockSpec((1, TK, dv), lambda bg, s, qt, kt, ft: (v_idx0 + bg % G, (bg // G) * nk + kt[s], 0)),
    ]
    args = [q, k, v]
    if mode == "sel":
        nsp = selneg.shape[-1]
        in_specs.append(pl.BlockSpec((1, TQ, nsp), lambda bg, s, qt, kt, ft: (bg % G, (bg // G) * nq + qt[s], 0)))
        args.append(selneg)
    if mode in ("sel", "win"):
        in_specs.append(pl.BlockSpec((R, 1, TQ, TK), lambda bg, s, qt, kt, ft: (bg % G, ft[s] >> 4, 0, 0)))
        args.append(strips)
    grid_spec = pltpu.PrefetchScalarGridSpec(
        num_scalar_prefetch=3,
        grid=(B * G, len(qt)),
        in_specs=in_specs,
        out_specs=pl.BlockSpec((R, TQ, dv), lambda bg, s, qt, kt, ft: (bg % G, (bg // G) * nq + qt[s], 0)),
        scratch_shapes=_flash_scratch(R * TQ, dv),
    )
    return pl.pallas_call(
        functools.partial(_flash_kernel, mode=mode, rows_per_tok=1),
        grid_spec=grid_spec,
        out_shape=jax.ShapeDtypeStruct((n_heads_out, dm.Mp, dv), F32),
        compiler_params=_params(("arbitrary", "arbitrary")),
        name="flash_" + mode,
    )(jnp.asarray(qt), jnp.asarray(kt), jnp.asarray(ft), *args)


def flash_mla_call(dm, q_lat, q_pe, ckv, kpe):
    B, T, M = dm.B, dm.T, dm.M
    nq, nk = T // TQ, T // TK
    qt, kt, ft = _steps(T, window=False)
    rows = TQ * H_C
    grid_spec = pltpu.PrefetchScalarGridSpec(
        num_scalar_prefetch=3,
        grid=(B, len(qt)),
        in_specs=[
            pl.BlockSpec((rows, KV_LORA), lambda b, s, qt, kt, ft: (b * nq + qt[s], 0)),
            pl.BlockSpec((rows, D_ROPE), lambda b, s, qt, kt, ft: (b * nq + qt[s], 0)),
            pl.BlockSpec((TK, KV_LORA), lambda b, s, qt, kt, ft: (b * nk + kt[s], 0)),
            pl.BlockSpec((TK, D_ROPE), lambda b, s, qt, kt, ft: (b * nk + kt[s], 0)),
        ],
        out_specs=pl.BlockSpec((rows, KV_LORA), lambda b, s, qt, kt, ft: (b * nq + qt[s], 0)),
        scratch_shapes=_flash_scratch(rows, KV_LORA),
    )
    return pl.pallas_call(
        functools.partial(_flash_kernel, mode="mla", rows_per_tok=H_C),
        grid_spec=grid_spec,
        out_shape=jax.ShapeDtypeStruct((dm.Mp * H_C, KV_LORA), BF16),
        compiler_params=_params(("arbitrary", "arbitrary")),
        name="flash_mla",
    )(jnp.asarray(qt), jnp.asarray(kt), jnp.asarray(ft), q_lat, q_pe, ckv, kpe)


def _out_even_kernel(x_ref, gp_ref, gs_ref, gf_ref, *refs, n_p):
    w_ref, o_ref = refs[8], refs[9]
    i = pl.program_id(0)
    oc, os_, ow, ob = [lambda h, p=refs[k], s=refs[4 + k]: jnp.where(i < n_p, p[h], s[h]) for k in range(4)]
    gf = gf_ref[...]
    acc = jnp.zeros(x_ref.shape, F32)
    for h in range(H_A):
        a = (gf[:, H_B + h:H_B + h + 1] * oc(h)
             + gf[:, H_B + H_A + h:H_B + H_A + h + 1] * os_(h)
             + gf[:, H_B + 2 * H_A + h:H_B + 2 * H_A + h + 1] * ow(h))
        acc = acc + _dot(a, w_ref[h])
    for h in range(H_B):
        acc = acc + _dot(ob(h), w_ref[H_A + h])
    o_ref[...] = x_ref[...] + _pick_mod(i, n_p, gp_ref, gs_ref) * acc


def out_even_call(dm, x, gate, gf, o_prompt, o_sample, w_out_h):
    tm, M, n_p = dm.tm, dm.M, dm.n_p
    sp, ss = dm.mod_specs()
    hm_p = pl.BlockSpec((H_A, tm, HEAD_DIM), lambda i: (0, jnp.minimum(i, n_p - 1), 0))
    hm_s = pl.BlockSpec((H_A, tm, HEAD_DIM), lambda i: (0, jnp.maximum(i - n_p, 0), 0))
    return pl.pallas_call(
        functools.partial(_out_even_kernel, n_p=n_p),
        grid=(dm.n_t,),
        in_specs=[pl.BlockSpec((tm, D_MODEL), lambda i: (i, 0)), sp, ss,
                  pl.BlockSpec((tm, 128), lambda i: (i, 0))] + [hm_p] * 4 + [hm_s] * 4 + [
                  pl.BlockSpec((H_A + H_B, HEAD_DIM, D_MODEL), lambda i: (0, 0, 0))],
        out_specs=pl.BlockSpec((tm, D_MODEL), lambda i: (i, 0)),
        out_shape=jax.ShapeDtypeStruct((M, D_MODEL), F32),
        compiler_params=_params(("arbitrary",)),
        name="out_even",
    )(x, gate[0], gate[1], gf, *o_prompt, *o_sample, w_out_h)


def _proj_odd_kernel(x_ref, g_ref, shp_ref, shs_ref, scp_ref, scs_ref, w_ref, gq_ref, gkv_ref, cos_ref, sin_ref,
                     cqn_ref, ckv_ref, ckvb_ref, kpe_ref, kpeb_ref, *, n_p):
    i = pl.program_id(0)
    shift = _pick_mod(i, n_p, shp_ref, shs_ref)
    scale = _pick_mod(i, n_p, scp_ref, scs_ref)
    hb = _bf(_rms(x_ref[...], g_ref[...]) * (1.0 + scale) + shift)
    cq = jnp.dot(hb, w_ref[:, 0:Q_LORA], preferred_element_type=F32)
    cqn_ref[...] = _bf(_rms(cq, gq_ref[...]))
    ckv = _rms(jnp.dot(hb, w_ref[:, Q_LORA:Q_LORA + KV_LORA], preferred_element_type=F32), gkv_ref[...])
    ckv_ref[...] = ckv
    ckvb_ref[...] = _bf(ckv)
    pe = jnp.dot(hb, w_ref[:, Q_LORA + KV_LORA:Q_LORA + KV_LORA + 128], preferred_element_type=F32)
    kpe = pe[:, 0:D_ROPE] * cos_ref[...] + pe[:, D_ROPE:2 * D_ROPE] * sin_ref[...]
    kpe_ref[...] = kpe
    kpeb_ref[...] = _bf(kpe)


def proj_odd_call(dm, x, g, shift, scale, w_perm, g_q, g_kv, cos32, sin32):
    tm, M = dm.tm, dm.M
    sp, ss = dm.mod_specs()
    row = lambda w: pl.BlockSpec((tm, w), lambda i: (i, 0))
    const = lambda r, c: pl.BlockSpec((r, c), lambda i: (0, 0))
    return pl.pallas_call(
        functools.partial(_proj_odd_kernel, n_p=dm.n_p),
        grid=(dm.n_t,),
        in_specs=[row(D_MODEL), const(1, D_MODEL), sp, ss, sp, ss, const(D_MODEL, 896),
                  const(1, Q_LORA), const(1, KV_LORA), row(D_ROPE), row(D_ROPE)],
        out_specs=[row(Q_LORA), row(KV_LORA), row(KV_LORA), row(D_ROPE), row(D_ROPE)],
        out_shape=[jax.ShapeDtypeStruct((M, Q_LORA), BF16), jax.ShapeDtypeStruct((M, KV_LORA), F32),
                   jax.ShapeDtypeStruct((M, KV_LORA), BF16), jax.ShapeDtypeStruct((M, D_ROPE), F32),
                   jax.ShapeDtypeStruct((M, D_ROPE), BF16)],
        compiler_params=_params(("arbitrary",)),
        name="proj_odd",
    )(x, g, shift[0], shift[1], scale[0], scale[1], w_perm, g_q, g_kv, cos32, sin32)


def _bmm_nt_kernel(a_ref, b_ref, o_ref):
    o_ref[0] = _dot3_nt(a_ref[0], b_ref[0])


def _bmm_kernel(a_ref, b_ref, o_ref):
    o_ref[0] = _dot3(a_ref[0], b_ref[0])


def bmm_call(a, b, nt, name):
    H, m, _ = a.shape
    n = b.shape[1] if nt else b.shape[2]
    return pl.pallas_call(
        _bmm_nt_kernel if nt else _bmm_kernel,
        grid=(H,),
        in_specs=[pl.BlockSpec((1,) + a.shape[1:], lambda h: (h, 0, 0)),
                  pl.BlockSpec((1,) + b.shape[1:], lambda h: (h, 0, 0))],
        out_specs=pl.BlockSpec((1, m, n), lambda h: (h, 0, 0)),
        out_shape=jax.ShapeDtypeStruct((H, m, n), F32),
        compiler_params=_params(("arbitrary",)),
        name=name,
    )(a, b)


def _qlat_kernel(a_ref, w_ref, o_ref):
    o_ref[...] = _bf(jnp.dot(a_ref[...], w_ref[...], preferred_element_type=F32) * MLA_SCALE)


def _qpe_kernel(a_ref, w_ref, cos_ref, sin_ref, o_ref):
    y = jnp.dot(a_ref[...], w_ref[...], preferred_element_type=F32)
    n = o_ref.shape[1]
    o_ref[...] = _bf((y[:, 0:n] * cos_ref[...] + y[:, n:2 * n] * sin_ref[...]) * MLA_SCALE)


def q_mla_call(dm, cqn, w_qabs, w_qpe, cos_h, sin_h):
    tm, M = dm.tm, dm.M
    n_lat = H_C * KV_LORA
    n_pe = H_C * D_ROPE
    tn = 1024
    q_lat = pl.pallas_call(
        _qlat_kernel,
        grid=(dm.n_t, n_lat // tn),
        in_specs=[pl.BlockSpec((tm, Q_LORA), lambda i, j: (i, 0)),
                  pl.BlockSpec((Q_LORA, tn), lambda i, j: (0, j))],
        out_specs=pl.BlockSpec((tm, tn), lambda i, j: (i, j)),
        out_shape=jax.ShapeDtypeStruct((M, n_lat), BF16),
        compiler_params=_params(("arbitrary", "arbitrary")),
        name="q_lat",
    )(cqn, w_qabs)
    q_pe = pl.pallas_call(
        _qpe_kernel,
        grid=(dm.n_t,),
        in_specs=[pl.BlockSpec((tm, Q_LORA), lambda i: (i, 0)),
                  pl.BlockSpec((Q_LORA, 2 * n_pe), lambda i: (0, 0)),
                  pl.BlockSpec((tm, n_pe), lambda i: (i, 0)),
                  pl.BlockSpec((tm, n_pe), lambda i: (i, 0))],
        out_specs=pl.BlockSpec((tm, n_pe), lambda i: (i, 0)),
        out_shape=jax.ShapeDtypeStruct((M, n_pe), BF16),
        compiler_params=_params(("arbitrary",)),
        name="q_pe",
    )(cqn, w_qpe, cos_h, sin_h)
    return q_lat, q_pe


def _out_odd_kernel(x_ref, gp_ref, gs_ref, op_ref, os_ref, w_ref, o_ref, *, n_p):
    i = pl.program_id(0)
    o_lat = jnp.where(i < n_p, op_ref[...], os_ref[...])
    y = jnp.dot(o_lat, w_ref[...], preferred_element_type=F32)
    o_ref[...] = x_ref[...] + _pick_mod(i, n_p, gp_ref, gs_ref) * y


def out_odd_call(dm, x, gate, o_lat_p, o_lat_s, w_comb):
    tm, M, n_p = dm.tm, dm.M, dm.n_p
    sp, ss = dm.mod_specs()
    kdim = H_C * KV_LORA
    return pl.pallas_call(
        functools.partial(_out_odd_kernel, n_p=n_p),
        grid=(dm.n_t,),
        in_specs=[pl.BlockSpec((tm, D_MODEL), lambda i: (i, 0)), sp, ss,
                  pl.BlockSpec((tm, kdim), lambda i: (jnp.minimum(i, n_p - 1), 0)),
                  pl.BlockSpec((tm, kdim), lambda i: (jnp.maximum(i - n_p, 0), 0)),
                  pl.BlockSpec((kdim, D_MODEL), lambda i: (0, 0))],
        out_specs=pl.BlockSpec((tm, D_MODEL), lambda i: (i, 0)),
        out_shape=jax.ShapeDtypeStruct((M, D_MODEL), F32),
        compiler_params=_params(("arbitrary",)),
        name="out_odd",
    )(x, gate[0], gate[1], o_lat_p, o_lat_s, w_comb)


def _moe_pre_kernel(x_ref, g_ref, shp_ref, shs_ref, scp_ref, scs_ref, wr_ref, br_ref, h_ref, gates_ref, *, n_p):
    i = pl.program_id(0)
    shift = _pick_mod(i, n_p, shp_ref, shs_ref)
    scale = _pick_mod(i, n_p, scp_ref, scs_ref)
    h = _rms(x_ref[...], g_ref[...]) * (1.0 + scale) + shift
    h_ref[...] = _bf(h)
    logits = _dot3(h, wr_ref[...]) + br_ref[...]
    ne = logits.shape[1]
    lane = lax.broadcasted_iota(jnp.int32, logits.shape, 1)
    work = logits
    un = jnp.zeros_like(logits)
    den = jnp.zeros((logits.shape[0], 1), F32)
    m0 = None
    for k in range(TOP_K):
        m = jnp.max(work, axis=-1, keepdims=True)
        idx = jnp.min(jnp.where(work == m, lane, ne), axis=-1, keepdims=True)
        hit = lane == idx
        if k == 0:
            m0 = m
        e = jnp.exp(m - m0)
        un = un + jnp.where(hit, e, 0.0)
        den = den + e
        work = jnp.where(hit, -jnp.inf, work)
    gates_ref[...] = un / den


def moe_pre_call(dm, x, g, shift, scale, w_router, b_router):
    tm, M = dm.tm, dm.M
    ne = w_router.shape[1]
    sp, ss = dm.mod_specs()
    return pl.pallas_call(
        functools.partial(_moe_pre_kernel, n_p=dm.n_p),
        grid=(dm.n_t,),
        in_specs=[pl.BlockSpec((tm, D_MODEL), lambda i: (i, 0)), pl.BlockSpec((1, D_MODEL), lambda i: (0, 0)),
                  sp, ss, sp, ss,
                  pl.BlockSpec((D_MODEL, ne), lambda i: (0, 0)), pl.BlockSpec((1, ne), lambda i: (0, 0))],
        out_specs=[pl.BlockSpec((tm, D_MODEL), lambda i: (i, 0)), pl.BlockSpec((tm, ne), lambda i: (i, 0))],
        out_shape=[jax.ShapeDtypeStruct((M, D_MODEL), BF16), jax.ShapeDtypeStruct((M, ne), F32)],
        compiler_params=_params(("arbitrary",)),
        name="moe_router",
    )(x, g, shift[0], shift[1], scale[0], scale[1], w_router, b_router)


def _moe_dense_kernel(h_ref, gates_ref, wgu_ref, bgu_ref, wdn_ref, bdn_ref, x_ref, gp_ref, gs_ref,
                      o_ref, acc_ref, *, n_p, d_ff):
    i = pl.program_id(0)
    e = pl.program_id(1)

    @pl.when(e == 0)
    def _():
        acc_ref[...] = jnp.zeros_like(acc_ref)

    gu = jnp.dot(h_ref[...], wgu_ref[0], preferred_element_type=F32) + bgu_ref[0]
    gate = jnp.minimum(gu[:, :d_ff], SWIGLU_LIMIT)
    up = jnp.clip(gu[:, d_ff:], -SWIGLU_LIMIT, SWIGLU_LIMIT)
    act = (up + 1.0) * gate * jax.nn.sigmoid(SWIGLU_ALPHA * gate)
    y = jnp.dot(_bf(act), wdn_ref[0], preferred_element_type=F32) + bdn_ref[0]
    gates = gates_ref[...]
    lane = lax.broadcasted_iota(jnp.int32, gates.shape, 1)
    g_e = jnp.sum(jnp.where(lane == e, gates, 0.0), axis=-1, keepdims=True)
    acc_ref[...] += g_e * y

    @pl.when(e == pl.num_programs(1) - 1)
    def _():
        o_ref[...] = x_ref[...] + _pick_mod(i, n_p, gp_ref, gs_ref) * acc_ref[...]


def moe_dense_call(dm, h, gates, w_gu, b_gu, w_dn, b_dn, x, gate):
    tm, M = dm.tm, dm.M
    ne, _, n2 = w_gu.shape
    d_ff = n2 // 2
    sp, ss = dm.mod_specs()
    return pl.pallas_call(
        functools.partial(_moe_dense_kernel, n_p=dm.n_p, d_ff=d_ff),
        grid=(dm.n_t, ne),
        in_specs=[pl.BlockSpec((tm, D_MODEL), lambda i, e: (i, 0)),
                  pl.BlockSpec((tm, ne), lambda i, e: (i, 0)),
                  pl.BlockSpec((1, D_MODEL, n2), lambda i, e: (e, 0, 0)),
                  pl.BlockSpec((1, 1, n2), lambda i, e: (e, 0, 0)),
                  pl.BlockSpec((1, d_ff, D_MODEL), lambda i, e: (e, 0, 0)),
                  pl.BlockSpec((1, 1, D_MODEL), lambda i, e: (e, 0, 0)),
                  pl.BlockSpec((tm, D_MODEL), lambda i, e: (i, 0)), sp, ss],
        out_specs=pl.BlockSpec((tm, D_MODEL), lambda i, e: (i, 0)),
        out_shape=jax.ShapeDtypeStruct((M, D_MODEL), F32),
        scratch_shapes=[pltpu.VMEM((tm, D_MODEL), F32)],
        compiler_params=_params(("arbitrary", "arbitrary")),
        name="moe_experts",
    )(h, gates, w_gu, b_gu.reshape(ne, 1, n2), w_dn, b_dn.reshape(ne, 1, D_MODEL), x, gate[0], gate[1])


def _final_norm_kernel(x_ref, g_ref, o_ref):
    o_ref[...] = _rms(x_ref[...], g_ref[...])


def final_norm_call(dm, x, g):
    tm = dm.tm
    return pl.pallas_call(
        _final_norm_kernel,
        grid=(dm.n_t,),
        in_specs=[pl.BlockSpec((tm, D_MODEL), lambda i: (i, 0)), pl.BlockSpec((1, D_MODEL), lambda i: (0, 0))],
        out_specs=pl.BlockSpec((tm, D_MODEL), lambda i: (i, 0)),
        out_shape=jax.ShapeDtypeStruct((dm.M, D_MODEL), F32),
        compiler_params=_params(("arbitrary",)),
        name="final_norm",
    )(x, g)


def _prep_even(w_in, b_f, w_cmp, pe_cmp, w_out):
    a_q, a_kv, a_g, b_q = H_A * HEAD_DIM, KV_W, 3 * H_A, H_B * HEAD_DIM
    o = np.cumsum([0, a_q, a_kv, a_kv, a_kv, a_g, b_q, KV_W, H_B])
    q_a, kv_cmp, kv_sel, kv_win, g_a, q_b, kv_b, f_b = [w_in[:, o[i]:o[i + 1]] for i in range(8)]
    pad = jnp.zeros((D_MODEL, 128 - H_B - a_g), w_in.dtype)
    w_perm = _bf(jnp.concatenate([q_a, q_b, kv_cmp, kv_sel, kv_win, kv_b, f_b, g_a, pad], axis=1))
    bf_row = jnp.concatenate([b_f, jnp.zeros((128 - H_B,), F32)])[None]
    eye = jnp.eye(2, dtype=F32)
    w_big = _bf(jnp.einsum('kldo,kK,gG->lkgdKGo', w_cmp, eye, eye).reshape(L_CMP * KV_W, KV_W))
    pe_flat = jnp.broadcast_to(jnp.transpose(pe_cmp, (1, 0, 2))[:, :, None, :],
                               (L_CMP, 2, G_A, HEAD_DIM)).reshape(1, L_CMP * KV_W)
    w_out_h = _bf(w_out.reshape(H_A + H_B, HEAD_DIM, D_MODEL))
    return w_perm, bf_row, w_big, pe_flat, w_out_h


def _rot_cols(w):
    half = D_ROPE // 2
    return jnp.concatenate([-w[..., half:], w[..., :half]], axis=-1)


def _prep_odd(w_in, w_qb, w_kvb, w_out):
    kpe_w = w_in[:, Q_LORA + KV_LORA:]
    pad = jnp.zeros((D_MODEL, 128 - 2 * D_ROPE), w_in.dtype)
    w_perm = _bf(jnp.concatenate([w_in[:, :Q_LORA + KV_LORA], kpe_w, _rot_cols(kpe_w), pad], axis=1))
    wq = w_qb.reshape(Q_LORA, H_C, D_NOPE + D_ROPE)
    wkv = w_kvb.reshape(KV_LORA, H_C, D_NOPE + D_V)
    nope_h = jnp.transpose(wq[..., :D_NOPE], (1, 0, 2))
    w_uk_h = jnp.transpose(wkv[..., :D_NOPE], (1, 0, 2))
    w_qabs = bmm_call(nope_h, w_uk_h, True, "w_q_absorb")
    w_qabs = _bf(jnp.transpose(w_qabs, (1, 0, 2)).reshape(Q_LORA, H_C * KV_LORA))
    pe_w = wq[..., D_NOPE:]
    w_qpe = _bf(jnp.concatenate([pe_w.reshape(Q_LORA, H_C * D_ROPE),
                                 _rot_cols(pe_w).reshape(Q_LORA, H_C * D_ROPE)], axis=1))
    w_uv_h = jnp.transpose(wkv[..., D_NOPE:], (1, 0, 2))
    w_comb = bmm_call(w_uv_h, w_out.reshape(H_C, D_V, D_MODEL), False, "w_out_absorb")
    w_comb = _bf(w_comb.reshape(H_C * KV_LORA, D_MODEL))
    return w_perm, w_qabs, w_qpe, w_comb


def _rope_tables(dm, past):
    half = D_ROPE // 2
    inv = ROPE_THETA ** (-jnp.arange(half, dtype=F32) / half)
    pos = jnp.concatenate([jnp.tile(jnp.arange(dm.T, dtype=jnp.int32), dm.B),
                           jnp.tile(past + jnp.arange(dm.DT, dtype=jnp.int32), dm.DB)])
    ang = pos.astype(F32)[:, None] * inv[None, :]
    cos32 = jnp.tile(jnp.cos(ang), (1, 2))
    sin32 = jnp.tile(jnp.sin(ang), (1, 2))
    return cos32, sin32, jnp.tile(cos32, (1, H_C)), jnp.tile(sin32, (1, H_C))


def _split3_cols(c):
    hi, mid, lo = _split3(c)
    return [hi, mid, lo]


def _fox_augment(dm, q_hm, kvh, cum):
    Mp, M = dm.Mp, dm.M
    cum = jnp.pad(cum.reshape(Mp, H_B), ((0, M - Mp), (0, 0)))
    ones = jnp.ones((M, 3), BF16)
    zeros3 = jnp.zeros((M, 3), BF16)
    qs, ks = [], []
    for g in range(G_B):
        kcols = [kvh[8 + g]]
        for r in range(R_B):
            kcols += _split3_cols(-cum[:, g * R_B + r:g * R_B + r + 1])
        kcols += [ones, jnp.zeros((M, 128 - HEAD_DIM - 3 * R_B - 3), BF16)]
        ks.append(jnp.concatenate(kcols, axis=1))
        for r in range(R_B):
            h = g * R_B + r
            qcols = [q_hm[H_A + h]] + [ones if rr == r else zeros3 for rr in range(R_B)]
            qcols += _split3_cols(cum[:, h:h + 1])
            qcols += [jnp.zeros((M, 128 - HEAD_DIM - 3 * R_B - 3), BF16)]
            qs.append(jnp.concatenate(qcols, axis=1))
    return jnp.stack(qs), jnp.stack(ks)


NEW_PAD = 8
NEW_LANES = 128


def _page_copy(pool_ref, buf_ref, sem, pt_ref, b, slot, j, rows):
    return pltpu.make_async_copy(pool_ref.at[pt_ref[b, j]], buf_ref.at[slot, pl.ds(j * rows, rows)], sem.at[slot])


def _fetch_pages(pools, pt_ref, n_pages):
    b = pl.program_id(0)
    nb = pl.num_programs(0)
    slot = b % 2

    def start(bb, sl):
        for pool_ref, buf_ref, sem, rows in pools:
            for j in range(n_pages):
                _page_copy(pool_ref, buf_ref, sem, pt_ref, bb, sl, j, rows).start()

    @pl.when(b == 0)
    def _():
        start(b, slot)

    @pl.when(b + 1 < nb)
    def _():
        start(b + 1, 1 - slot)

    for pool_ref, buf_ref, sem, rows in pools:
        for j in range(n_pages):
            _page_copy(pool_ref, buf_ref, sem, pt_ref, b, slot, j, rows).wait()
    return slot


def _softmax2(lg_p, lg_n):
    m = jnp.maximum(jnp.max(lg_p, axis=-1, keepdims=True), jnp.max(lg_n, axis=-1, keepdims=True))
    e_p = jnp.exp(lg_p - m)
    e_n = jnp.exp(lg_n - m)
    s = jnp.sum(e_p, axis=-1, keepdims=True) + jnp.sum(e_n, axis=-1, keepdims=True)
    return e_p / s, e_n / s


def _t5_decode_kernel(tabt_ref, o_ref, *, kind, past, n_past, dt, n_new):
    rows, lanes = o_ref.shape
    r = lax.broadcasted_iota(jnp.int32, (rows, lanes), 0)
    l = lax.broadcasted_iota(jnp.int32, (rows, lanes), 1)
    qpos = past + r // H_A
    u = l - n_past
    if kind == "sel":
        kpos = jnp.where(l < n_past, l, past + u)
        ok = kpos <= qpos
    elif kind == "win":
        kpos = jnp.where(l < n_past, past - n_past + l, past + u)
        ok = (kpos <= qpos) & (qpos - kpos < WINDOW) & (kpos >= 0)
    else:
        kpos = jnp.where(l < n_past, (l + 1) * L_CMP - 1, past + (u + 1) * L_CMP - 1)
        ok = kpos <= qpos
    ok = ok & (u < n_new)
    bucket = _t5_bucket(qpos - kpos)
    acc = jnp.zeros((rows, lanes), F32)
    for b in range(N_BUCKETS):
        acc = jnp.where(bucket == b, tabt_ref[:, b:b + 1], acc)
    o_ref[...] = jnp.where(ok, acc, NEG_BIG)


def t5_decode_call(t5_table, kind, past, n_past, dt, n_new):
    rows = dt * H_A
    tabt = jnp.tile(t5_table.T, (dt, 1))
    return pl.pallas_call(
        functools.partial(_t5_decode_kernel, kind=kind, past=past, n_past=n_past, dt=dt, n_new=n_new),
        out_shape=jax.ShapeDtypeStruct((rows, n_past + NEW_LANES), F32),
        compiler_params=_params(None),
        name="t5_decode_" + kind,
    )(tabt)


def _dec_cmp_kernel(pt_ref, q_ref, pool_ref, cnew_ref, bias_ref, o_ref, sel_ref, buf, sem, *,
                    n_pages, past, dt, n_new, n_sel_blocks, k_sel):
    slot = _fetch_pages([(pool_ref, buf, sem, PAGE_SIZE // L_CMP)], pt_ref, n_pages)
    ncp = buf.shape[1]
    q = q_ref[0]
    c = buf[slot]
    cn = cnew_ref[0]
    lg_p = _dot_nt(q, c[:, :128]) + bias_ref[:, :ncp]
    lg_n = _dot_nt(q, cn[:, :128]) + bias_ref[:, ncp:ncp + NEW_PAD]
    p_p, p_n = _softmax2(lg_p, lg_n)
    o_ref[0] = _dot(p_p, c[:, 128:]) + _dot(p_n, cn[:, 128:])
    n_tg = dt * G_A
    d = functools.partial(jnp.dot, preferred_element_type=F32)
    nsp = sel_ref.shape[2]

    def group_sum(p):
        return jnp.concatenate([jnp.sum(p[i * R_A:(i + 1) * R_A], axis=0, keepdims=True) for i in range(n_tg)],
                               axis=0)

    pair = (lax.broadcasted_iota(jnp.int32, (ncp, nsp), 0) * L_CMP // L_SEL
            == lax.broadcasted_iota(jnp.int32, (ncp, nsp), 1)).astype(BF16)
    hi, mid, lo = _split3(group_sum(p_p))
    imp_s = d(hi, pair) + d(mid, pair) + d(lo, pair)
    blk = lax.broadcasted_iota(jnp.int32, (n_tg, nsp), 1)
    imp_n = group_sum(p_n)
    for u in range(n_new):
        imp_s = imp_s + jnp.where(blk == (past + u * L_CMP) // L_SEL, imp_n[:, u:u + 1], 0.0)
    cur = (past + lax.broadcasted_iota(jnp.int32, (n_tg, nsp), 0) // G_A) // L_SEL
    valid = (blk <= cur) & (blk < n_sel_blocks)
    forced = (blk == 0) | (blk == cur) | (blk == cur - 1)
    score = jnp.where(forced & valid, jnp.inf, jnp.where(valid, imp_s, -jnp.inf))
    chosen = jnp.zeros((n_tg, nsp), jnp.bool_)
    for _ in range(k_sel):
        m = jnp.max(score, axis=-1, keepdims=True)
        idx = jnp.min(jnp.where(score == m, blk, nsp), axis=-1, keepdims=True)
        hit = blk == idx
        chosen = chosen | hit
        score = jnp.where(hit, -jnp.inf, score)
    sel_ref[0] = jnp.where(chosen, 0.0, SEL_NEG).astype(BF16)


def _dec_sel_kernel(pt_ref, q_ref, pool_ref, knew_ref, selneg_ref, bias_ref, o_ref, buf, sem, onehot, *,
                    n_pages, past, dt):
    b = pl.program_id(0)
    nsp, lanes = onehot.shape

    @pl.when(b == 0)
    def _():
        l = lax.broadcasted_iota(jnp.int32, (nsp, lanes), 1)
        onehot[...] = (lax.broadcasted_iota(jnp.int32, (nsp, lanes), 0) == l // L_SEL).astype(BF16)

    slot = _fetch_pages([(pool_ref, buf, sem, PAGE_SIZE)], pt_ref, n_pages)
    q = q_ref[0]
    sel_tg = selneg_ref[0].astype(F32)
    sel_rows = _bf(jnp.concatenate([jnp.broadcast_to(sel_tg[i:i + 1], (R_A, nsp)) for i in range(dt * G_A)],
                                   axis=0))
    sb = jnp.dot(sel_rows, onehot[...], preferred_element_type=F32) + bias_ref[...]
    kv = _bf(buf[slot])
    kn = _bf(knew_ref[0])
    lg_p = _dot_nt(q, kv[:, :128]) + sb[:, :past]
    lg_n = _dot_nt(q, kn[:, :128]) + sb[:, past:past + NEW_PAD]
    p_p, p_n = _softmax2(lg_p, lg_n)
    o_ref[0] = _dot(p_p, kv[:, 128:]) + _dot(p_n, kn[:, 128:])


def _dec_fox_kernel(pt_ref, q_ref, pool_ref, lf_ref, knew_ref, lfn_ref, o_ref, buf, sem, buflf, semlf, *,
                    n_pages, past, dt):
    slot = _fetch_pages([(pool_ref, buf, sem, PAGE_SIZE), (lf_ref, buflf, semlf, H_B)], pt_ref, n_pages)
    q = q_ref[0]
    rows = q.shape[0]
    x = buflf[slot]
    n = x.shape[0]
    tri = (lax.broadcasted_iota(jnp.int32, (PAGE_SIZE, PAGE_SIZE), 0)
           <= lax.broadcasted_iota(jnp.int32, (PAGE_SIZE, PAGE_SIZE), 1)).astype(BF16)
    d = functools.partial(jnp.dot, preferred_element_type=F32)

    def cum_lanes(v):
        hi, mid, lo = _split3(v)
        return d(hi, tri) + d(mid, tri) + d(lo, tri)

    w = cum_lanes(x)
    run = jnp.zeros((H_B, 1), F32)
    pieces = []
    for j in range(n_pages):
        wj = w[j * H_B:(j + 1) * H_B]
        pieces.append(wj + run)
        run = run + wj[:, PAGE_SIZE - 1:PAGE_SIZE]
    ck = jnp.concatenate(pieces, axis=1)
    cnew = cum_lanes(lfn_ref[0]) + run
    ck32 = jnp.concatenate([ck] * dt, axis=0)
    cn32 = jnp.concatenate([cnew[:, :NEW_PAD]] * dt, axis=0)
    cq32 = jnp.concatenate([cnew[:, t:t + 1] for t in range(dt)], axis=0)
    kv = _bf(buf[slot])
    kn = _bf(knew_ref[0])
    lg_p = _dot_nt(q, kv[:, :128]) + (cq32 - ck32)
    u = lax.broadcasted_iota(jnp.int32, (rows, NEW_PAD), 1)
    t = lax.broadcasted_iota(jnp.int32, (rows, NEW_PAD), 0) // H_B
    lg_n = jnp.where(u <= t, _dot_nt(q, kn[:, :128]) + (cq32 - cn32), NEG_BIG)
    p_p, p_n = _softmax2(lg_p, lg_n)
    o_ref[0] = _dot(p_p, kv[:, 128:]) + _dot(p_n, kn[:, 128:])


def _dec_win_kernel(q_ref, win_ref, knew_ref, bias_ref, o_ref):
    q = q_ref[0]
    kv = _bf(win_ref[0])
    kn = _bf(knew_ref[0])
    wb = kv.shape[0]
    lg_p = _dot_nt(q, kv[:, :128]) + bias_ref[:, :wb]
    lg_n = _dot_nt(q, kn[:, :128]) + bias_ref[:, wb:wb + NEW_PAD]
    p_p, p_n = _softmax2(lg_p, lg_n)
    o_ref[0] = _dot(p_p, kv[:, 128:]) + _dot(p_n, kn[:, 128:])


def _dec_mla_kernel(pt_ref, q_ref, qpe_ref, pool_ref, poolpe_ref, cnew_ref, penew_ref, o_ref,
                    buf, sem, bufpe, sempe, *, n_pages, layer):
    slot = _fetch_pages([(pool_ref.at[layer], buf, sem, PAGE_SIZE), (poolpe_ref.at[layer], bufpe, sempe, PAGE_SIZE)],
                        pt_ref, n_pages)
    q, qpe = q_ref[...], qpe_ref[...]
    rows = q.shape[0]
    c = _bf(buf[slot])
    cn = _bf(cnew_ref[0])
    lg_p = _dot_nt(q, c) + _dot_nt(qpe, bufpe[slot])
    u = lax.broadcasted_iota(jnp.int32, (rows, NEW_PAD), 1)
    t = lax.broadcasted_iota(jnp.int32, (rows, NEW_PAD), 0) // H_C
    lg_n = jnp.where(u <= t, _dot_nt(q, cn) + _dot_nt(qpe, penew_ref[0]), NEG_BIG)
    p_p, p_n = _softmax2(lg_p, lg_n)
    o_ref[...] = _bf(_dot(p_p, c) + _dot(p_n, cn))


def _paged_call(kern, name, dm, page_table, in_arrays, in_specs, out_shape, out_specs, scratch):
    grid_spec = pltpu.PrefetchScalarGridSpec(
        num_scalar_prefetch=1, grid=(dm.DB,), in_specs=in_specs, out_specs=out_specs, scratch_shapes=scratch)
    return pl.pallas_call(kern, grid_spec=grid_spec, out_shape=out_shape,
                          compiler_params=_params(("arbitrary",)), name=name)(page_table, *in_arrays)


_ANY = pl.BlockSpec(memory_space=pl.ANY)


def _seq_spec(*tail):
    return pl.BlockSpec((1,) + tail, lambda b, *_: (b,) + (0,) * len(tail))


def _const_spec(shape):
    return pl.BlockSpec(shape, lambda b, *_: (0,) * len(shape))


def _q_blockdiag(dm, q_heads):
    DB, DT = dm.DB, dm.DT
    q = jnp.transpose(q_heads.reshape(H_A, DB, DT, HEAD_DIM), (1, 2, 0, 3))
    g0 = (jnp.arange(H_A) < R_A)[None, None, :, None]
    z = jnp.zeros_like(q)
    return jnp.concatenate([jnp.where(g0, q, z), jnp.where(g0, z, q)], axis=-1).reshape(DB, DT * H_A, 2 * HEAD_DIM)


def _from_blockdiag(dm, o):
    DB, DT = dm.DB, dm.DT
    o = o.reshape(DB, DT, H_A, 2, HEAD_DIM)
    g0 = (jnp.arange(H_A) < R_A)[None, None, :, None]
    o = jnp.where(g0, o[..., 0, :], o[..., 1, :])
    return jnp.transpose(o, (2, 0, 1, 3)).reshape(H_A, DB * DT, HEAD_DIM)


def _new_rows(dm, a, width):
    return jnp.pad(a.reshape(dm.DB, dm.DT, width), ((0, 0), (0, NEW_PAD - dm.DT), (0, 0)))


def decode_even(dm, page_table, q_hm, kv_cmp, kv_sel, kv_win, kv_b, logf, pool_cmp, pool_sel, pool_fox_kv,
                pool_fox_logf, win_buf, pe_flat, w_big, dec_bias):
    DB, DT, Mp = dm.DB, dm.DT, dm.Mp
    P = pool_cmp.shape[0]
    n_pages = page_table.shape[1]
    past = n_pages * PAGE_SIZE
    total = past + DT
    padded = -(-total // L_SEL) * L_SEL
    ns = padded // L_SEL
    nsp = -(-ns // 128) * 128
    rows = DT * H_A
    bpp = PAGE_SIZE // L_CMP
    qa = _q_blockdiag(dm, q_hm[:H_A, Mp:])
    qb = _q_blockdiag(dm, q_hm[H_A:, Mp:])
    o_shape = jax.ShapeDtypeStruct((DB, rows, 2 * HEAD_DIM), F32)
    q_spec = _seq_spec(rows, 2 * HEAD_DIM)
    new_spec = _seq_spec(NEW_PAD, KV_W)
    bias_cmp, bias_sel, bias_win = dec_bias

    pool_rows = lax.optimization_barrier(pool_cmp.reshape(P, PAGE_SIZE, KV_W)).reshape(P * PAGE_SIZE, KV_W)
    pool_c, _ = compress_call(pool_rows, pe_flat, w_big)
    n_new = (padded - past) // L_CMP
    new_rows = jnp.pad(kv_cmp[Mp:].reshape(DB, DT * KV_W), ((0, 0), (0, (padded - total) * KV_W)))
    c_new, _ = compress_call(new_rows.reshape(DB * n_new * L_CMP, KV_W), pe_flat, w_big)
    c_new = jnp.pad(c_new.reshape(DB, n_new, KV_W), ((0, 0), (0, NEW_PAD - n_new), (0, 0)))
    o_cmp, selneg = _paged_call(
        functools.partial(_dec_cmp_kernel, n_pages=n_pages, past=past, dt=DT, n_new=n_new, n_sel_blocks=ns,
                          k_sel=min(N_SEL, ns)),
        "decode_cmp", dm, page_table,
        [qa, pool_c.reshape(P, bpp, KV_W), c_new, bias_cmp],
        [q_spec, _ANY, new_spec, _const_spec(bias_cmp.shape)],
        [o_shape, jax.ShapeDtypeStruct((DB, DT * G_A, nsp), BF16)],
        [q_spec, _seq_spec(DT * G_A, nsp)],
        [pltpu.VMEM((2, n_pages * bpp, KV_W), F32), pltpu.SemaphoreType.DMA((2,))])

    o_sel = _paged_call(
        functools.partial(_dec_sel_kernel, n_pages=n_pages, past=past, dt=DT),
        "decode_sel", dm, page_table,
        [qa, pool_sel.reshape(P, PAGE_SIZE, KV_W), _new_rows(dm, kv_sel[Mp:], KV_W), selneg, bias_sel],
        [q_spec, _ANY, new_spec, _seq_spec(DT * G_A, nsp), _const_spec(bias_sel.shape)],
        o_shape, q_spec,
        [pltpu.VMEM((2, past, KV_W), F32), pltpu.SemaphoreType.DMA((2,)),
         pltpu.VMEM((nsp, past + NEW_LANES), BF16)])

    lf_pool_t = jnp.transpose(pool_fox_logf, (0, 2, 1))
    lf_new_t = jnp.pad(jnp.transpose(logf[Mp:].reshape(DB, DT, H_B), (0, 2, 1)),
                       ((0, 0), (0, 0), (0, PAGE_SIZE - DT)))
    o_fox = _paged_call(
        functools.partial(_dec_fox_kernel, n_pages=n_pages, past=past, dt=DT),
        "decode_fox", dm, page_table,
        [qb, pool_fox_kv.reshape(P, PAGE_SIZE, KV_W), lf_pool_t, _new_rows(dm, kv_b[Mp:], KV_W), lf_new_t],
        [q_spec, _ANY, _ANY, new_spec, _seq_spec(H_B, PAGE_SIZE)],
        o_shape, q_spec,
        [pltpu.VMEM((2, past, KV_W), F32), pltpu.SemaphoreType.DMA((2,)),
         pltpu.VMEM((2, n_pages * H_B, PAGE_SIZE), F32), pltpu.SemaphoreType.DMA((2,))])

    wb = win_buf.shape[1]
    o_win = pl.pallas_call(
        _dec_win_kernel,
        grid=(DB,),
        in_specs=[q_spec, _seq_spec(wb, KV_W), new_spec, _const_spec(bias_win.shape)],
        out_specs=q_spec,
        out_shape=o_shape,
        compiler_params=_params(("arbitrary",)),
        name="decode_win",
    )(qa, win_buf.reshape(DB, wb, KV_W), _new_rows(dm, kv_win[Mp:], KV_W), bias_win)
    return tuple(_from_blockdiag(dm, o) for o in (o_cmp, o_sel, o_win, o_fox))


def decode_odd(dm, page_table, q_lat, q_pe, ckv, kpe_b, pool_ckv, pool_kpe, layer):
    DB, DT, Mp = dm.DB, dm.DT, dm.Mp
    n_pages = page_table.shape[1]
    past = n_pages * PAGE_SIZE
    rows = DT * H_C
    row0 = Mp * H_C // rows
    q_spec = lambda w: pl.BlockSpec((rows, w), lambda b, *_: (row0 + b, 0))
    return _paged_call(
        functools.partial(_dec_mla_kernel, n_pages=n_pages, layer=layer),
        "decode_mla", dm, page_table,
        [q_lat, q_pe, pool_ckv, pool_kpe, _new_rows(dm, ckv[Mp:], KV_LORA), _new_rows(dm, kpe_b[Mp:], D_ROPE)],
        [q_spec(KV_LORA), q_spec(D_ROPE), _ANY, _ANY, _seq_spec(NEW_PAD, KV_LORA), _seq_spec(NEW_PAD, D_ROPE)],
        jax.ShapeDtypeStruct((DB * rows, KV_LORA), BF16),
        pl.BlockSpec((rows, KV_LORA), lambda b, *_: (b, 0)),
        [pltpu.VMEM((2, past, KV_LORA), F32), pltpu.SemaphoreType.DMA((2,)),
         pltpu.VMEM((2, past, D_ROPE), F32), pltpu.SemaphoreType.DMA((2,))])


def kernel(x_prompt, x_sample, c_prompt, c_sample, cache_nsa_cmp, cache_nsa_sel, cache_fox_kv, cache_fox_logf, state_nsa_win, cache_mla_ckv, cache_mla_kpe, page_table, t5_table, w_in_ab, b_forget, w_cmp, pe_cmp, w_out_ab, w_in_c, g_qnorm, w_qb, g_kvnorm, w_kvb, w_out_c, g_norm_mix, g_norm_ffn, w_ada, b_ada, w_router, b_router, w_moe_gu, b_moe_gu, w_moe_down, b_moe_down, g_final):
    B, T, _ = x_prompt.shape
    DB, DT, _ = x_sample.shape
    dm = Dims(B, T, DB, DT)
    Mp, M = dm.Mp, dm.M
    depth = w_ada.shape[0]
    past = page_table.shape[1] * PAGE_SIZE
    kv5 = (2, G_A, HEAD_DIM)

    x = jnp.concatenate([x_prompt.reshape(Mp, D_MODEL), x_sample.reshape(dm.Ms, D_MODEL)], axis=0)
    mod = ada_call(jnp.concatenate([c_prompt, c_sample], axis=0), w_ada, b_ada)

    def mod_vec(layer, k):
        m = mod[layer, :, k * D_MODEL:(k + 1) * D_MODEL]
        return m[:B].reshape(B, 1, D_MODEL), jnp.repeat(m[B:], DT, axis=0)

    strips = t5_strips_call(t5_table)
    cos32, sin32, cos_h, sin_h = _rope_tables(dm, past)
    n_pages = page_table.shape[1]
    wb = state_nsa_win.shape[2]
    n_new_cmp = (-(-(past + DT) // L_SEL) * L_SEL - past) // L_CMP
    dec_bias = (t5_decode_call(t5_table, "cmp", past, n_pages * (PAGE_SIZE // L_CMP), DT, n_new_cmp),
                t5_decode_call(t5_table, "sel", past, past, DT, DT),
                t5_decode_call(t5_table, "win", past, wb, DT, DT))

    even_p, even_s, odd_p, odd_s = [], [], [], []
    for layer in range(depth):
        shift0, scale0, gate0, shift1, scale1, gate1 = [mod_vec(layer, k) for k in range(6)]
        g_mix = g_norm_mix[layer][None]
        if layer % 2 == 0:
            e = layer // 2
            w_perm, bf_row, w_big, pe_flat, w_out_h = _prep_even(w_in_ab[e], b_forget[e], w_cmp[e], pe_cmp[e],
                                                                 w_out_ab[e])
            kv_cmp, kv_sel, kv_win, kv_b, gf, logf, q_hm, kvh = proj_even_call(dm, x, g_mix, shift0, scale0,
                                                                              w_perm, bf_row)
            _, cmp_hm = compress_call(kv_cmp[:Mp], pe_flat, w_big)
            o_cmp, selneg = cmp_select_prompt_call(dm, t5_table, q_hm, cmp_hm)
            o_sel = flash_gqa_call(dm, "sel", q_hm, 0, kvh, 0, kvh, 2, H_A, selneg=selneg, strips=strips)
            o_win = flash_gqa_call(dm, "win", q_hm, 0, kvh, 4, kvh, 6, H_A, strips=strips)
            logf_t = jnp.transpose(logf[:Mp].reshape(B, T, H_B), (0, 2, 1)).reshape(B * H_B, T)
            cum = jnp.transpose(cumsum_call(logf_t).reshape(B, H_B, T), (0, 2, 1))
            q_fox, k_fox = _fox_augment(dm, q_hm, kvh, cum)
            o_fox = flash_gqa_call(dm, "fox", q_fox, 0, k_fox, 0, kvh, 10, H_B)
            d_cmp, d_sel, d_win, d_fox = decode_even(
                dm, page_table, q_hm, kv_cmp, kv_sel, kv_win, kv_b, logf, cache_nsa_cmp[e], cache_nsa_sel[e],
                cache_fox_kv[e], cache_fox_logf[e], state_nsa_win[e], pe_flat, w_big, dec_bias)
            win_new = jnp.concatenate([state_nsa_win[e], kv_win[Mp:].reshape((DB, DT) + kv5)], axis=1)[:, DT:]
            x = out_even_call(dm, x, gate0, gf, (o_cmp, o_sel, o_win, o_fox), (d_cmp, d_sel, d_win, d_fox), w_out_h)
            n_keep = min(WINDOW, T)
            sp = lambda a: a[:Mp].reshape((B, T) + kv5)
            ss = lambda a: a[Mp:].reshape((DB, DT) + kv5)
            even_p.append((sp(kv_cmp), sp(kv_sel), sp(kv_b), logf[:Mp].reshape(B, T, H_B),
                           sp(kv_win)[:, T - n_keep:]))
            even_s.append((ss(kv_cmp), ss(kv_sel), ss(kv_b), logf[Mp:].reshape(DB, DT, H_B), win_new))
        else:
            o = layer // 2
            w_perm, w_qabs, w_qpe, w_comb = _prep_odd(w_in_c[o], w_qb[o], w_kvb[o], w_out_c[o])
            cqn, ckv, ckv_b, kpe, kpe_b = proj_odd_call(dm, x, g_mix, shift0, scale0, w_perm,
                                                        g_qnorm[o][None], g_kvnorm[o][None], cos32, sin32)
            q_lat, q_pe = q_mla_call(dm, cqn, w_qabs, w_qpe, cos_h, sin_h)
            q_lat = q_lat.reshape(M * H_C, KV_LORA)
            q_pe = q_pe.reshape(M * H_C, D_ROPE)
            o_lat = flash_mla_call(dm, q_lat, q_pe, ckv_b, kpe_b)
            d_lat = decode_odd(dm, page_table, q_lat, q_pe, ckv, kpe_b, cache_mla_ckv, cache_mla_kpe, o)
            x = out_odd_call(dm, x, gate0, o_lat.reshape(Mp, H_C * KV_LORA),
                             d_lat.reshape(dm.Ms, H_C * KV_LORA), w_comb)
            odd_p.append((ckv[:Mp].reshape(B, T, KV_LORA), kpe[:Mp].reshape(B, T, D_ROPE)))
            odd_s.append((ckv[Mp:].reshape(DB, DT, KV_LORA), kpe[Mp:].reshape(DB, DT, D_ROPE)))
        h, gates = moe_pre_call(dm, x, g_norm_ffn[layer][None], shift1, scale1, w_router[layer],
                                b_router[layer][None])
        x = moe_dense_call(dm, h, gates, _bf(w_moe_gu[layer]), b_moe_gu[layer], _bf(w_moe_down[layer]),
                           b_moe_down[layer], x, gate1)

    y = final_norm_call(dm, x, g_final[None])
    outs = [y[:Mp].reshape(B, T, D_MODEL), y[Mp:].reshape(DB, DT, D_MODEL)]
    ep = [jnp.stack(a) for a in zip(*even_p)]
    es = [jnp.stack(a) for a in zip(*even_s)]
    op = [jnp.stack(a) for a in zip(*odd_p)]
    os_ = [jnp.stack(a) for a in zip(*odd_s)]
    for a, b in zip(ep, es):
        outs += [a, b]
    for a, b in zip(op, os_):
        outs += [a, b]
    return tuple(outs)
```
